```python
import math
import jax, jax.numpy as jnp
from jax import lax
import numpy as np

D_MODEL = 2048
BATCH = 2
SEQ = 4096
DEPTH = 1
DEC_BATCH = 32
DEC_SEQ = 8
PAST_LEN = 16384
PAGE_SIZE = 128

MIX_WIDTH = D_MODEL
ATTN_WIDTH = MIX_WIDTH // 2
CONV_CH = MIX_WIDTH - ATTN_WIDTH
HEAD_DIM = 128
N_HEADS = ATTN_WIDTH // HEAD_DIM
CONV_KERNEL = 31
PLE_DIM = 256
N_GROUPS = 4
EXPERTS_PER_GROUP = 8
N_EXPERTS = N_GROUPS * EXPERTS_PER_GROUP
TOP_K_INNER = 2
D_EXPERT = D_MODEL // 4
Q_BLOCK = 128
EPS = 1e-6
SB_BIAS_INIT = -8.0
IN_COLS = 3 * ATTN_WIDTH + 2 * CONV_CH

kernel_name = 'hybrid_stickbreak_conformer_hmoe_step'


def rms_norm(x, g):
    xf = x.astype(jnp.float32)
    y = xf * lax.rsqrt(jnp.mean(xf * xf, axis=-1, keepdims=True) + EPS)
    return (y * g.astype(jnp.float32)).astype(x.dtype)


def layer_norm(x, g, b):
    xf = x.astype(jnp.float32)
    mu = jnp.mean(xf, axis=-1, keepdims=True)
    xc = xf - mu
    var = jnp.mean(xc * xc, axis=-1, keepdims=True)
    y = xc * lax.rsqrt(var + EPS) * g.astype(jnp.float32) + b.astype(jnp.float32)
    return y.astype(x.dtype)


def mix_inputs(h, g_norm, w_in, g_q, g_k):
    bsz, t = h.shape[0], h.shape[1]
    z = rms_norm(h, g_norm) @ w_in
    q = z[..., :ATTN_WIDTH].reshape(bsz, t, N_HEADS, HEAD_DIM)
    k = z[..., ATTN_WIDTH:2 * ATTN_WIDTH].reshape(bsz, t, N_HEADS, HEAD_DIM)
    v = z[..., 2 * ATTN_WIDTH:3 * ATTN_WIDTH].reshape(bsz, t, N_HEADS, HEAD_DIM)
    u = z[..., 3 * ATTN_WIDTH:]
    return rms_norm(q, g_q), rms_norm(k, g_k), v, u


def stick_breaking_prompt(q, k, v, sb_bias):
    bsz, s, h, d = q.shape
    nb = s // Q_BLOCK
    qf = q.astype(jnp.float32) * (d ** -0.5)
    kf = k.astype(jnp.float32)
    vf = v.astype(jnp.float32)
    bias = sb_bias.astype(jnp.float32)[None, :, None, None]
    q_blocks = qf.reshape(bsz, nb, Q_BLOCK, h, d).transpose(1, 0, 2, 3, 4)
    key_pos = jnp.arange(s)

    def block(args):
        qb, start = args
        z = jnp.einsum('bqhd,bkhd->bhqk', qb, kf) + bias
        q_pos = start + jnp.arange(Q_BLOCK)
        mask = key_pos[None, :] < q_pos[:, None]
        l = jnp.where(mask, jax.nn.log_sigmoid(-z), 0.0)
        r = lax.cumsum(l, axis=3, reverse=True) - l
        a = jnp.where(mask, jnp.exp(jax.nn.log_sigmoid(z) + r), 0.0)
        return jnp.einsum('bhqk,bkhd->bqhd', a, vf)

    starts = jnp.arange(nb) * Q_BLOCK
    o = lax.map(block, (q_blocks, starts))
    return o.transpose(1, 0, 2, 3, 4).reshape(bsz, s, h, d).astype(v.dtype)


def stick_breaking_sample(q, k_new, v_new, cache_k, cache_v, layer, page_table, sb_bias):
    t = q.shape[1]
    qf = q.astype(jnp.float32) * (HEAD_DIM ** -0.5)
    kn = k_new.astype(jnp.float32)
    vn = v_new.astype(jnp.float32)
    bias = sb_bias.astype(jnp.float32)[None, :, None, None]
    z = jnp.einsum('bqhd,bkhd->bhqk', qf, kn) + bias
    mask = jnp.arange(t)[None, :] < jnp.arange(t)[:, None]
    l = jnp.where(mask, jax.nn.log_sigmoid(-z), 0.0)
    r = lax.cumsum(l, axis=3, reverse=True) - l
    a = jnp.where(mask, jnp.exp(jax.nn.log_sigmoid(z) + r), 0.0)
    o0 = jnp.einsum('bhqk,bkhd->bqhd', a, vn)
    c0 = jnp.sum(l, axis=-1)

    def page_step(carry, pt):
        o, c = carry
        kp = cache_k[layer, pt].astype(jnp.float32)
        vp = cache_v[layer, pt].astype(jnp.float32)
        zp = jnp.einsum('bqhd,bphd->bhqp', qf, kp) + bias
        lp = jax.nn.log_sigmoid(-zp)
        rp = c[..., None] + lax.cumsum(lp, axis=3, reverse=True) - lp
        ap = jnp.exp(jax.nn.log_sigmoid(zp) + rp)
        o = o + jnp.einsum('bhqp,bphd->bqhd', ap, vp)
        return (o, c + jnp.sum(lp, axis=-1)), None

    (o, _), _ = lax.scan(page_step, (o0, c0), page_table[:, ::-1].T)
    return o.astype(v_new.dtype)


def conformer_conv(u_in, hist, w_dw, b_dw, ln_g, ln_b, beta):
    ga, gb = jnp.split(u_in, 2, axis=-1)
    u = ga * jax.nn.sigmoid(gb)
    full = jnp.concatenate([hist.astype(u.dtype), u], axis=1)
    y = lax.conv_general_dilated(full, w_dw[:, None, :].astype(u.dtype), window_strides=(1,),
                                 padding='VALID', dimension_numbers=('NWC', 'WIO', 'NWC'),
                                 feature_group_count=CONV_CH)
    y = layer_norm(y + b_dw, ln_g, ln_b)
    y = jax.nn.silu(y) * beta
    return y, full[:, -(CONV_KERNEL - 1):]


def hier_moe(x, w_grp, b_grp, w_rt, b_rt, w_gate, w_up, w_down):
    bsz, t, d = x.shape
    xt = x.reshape(bsz * t, d)
    xf = xt.astype(jnp.float32)
    grp_logits = xf @ w_grp.astype(jnp.float32) + b_grp.astype(jnp.float32)
    grp_prob = jax.nn.softmax(grp_logits, axis=-1)
    g_onehot = jax.nn.one_hot(jnp.argmax(grp_logits, axis=-1), N_GROUPS, dtype=jnp.float32)
    g_w = jnp.sum(grp_prob * g_onehot, axis=-1, keepdims=True)
    e_logits = (xf @ w_rt.astype(jnp.float32) + b_rt.astype(jnp.float32)).reshape(-1, N_GROUPS, EXPERTS_PER_GROUP)
    sel = jnp.einsum('tge,tg->te', e_logits, g_onehot)
    top_v, top_i = lax.top_k(sel, TOP_K_INNER)
    top_w = jax.nn.softmax(top_v, axis=-1) * g_w
    inner = jnp.sum(jax.nn.one_hot(top_i, EXPERTS_PER_GROUP, dtype=jnp.float32) * top_w[..., None], axis=1)
    combine = (g_onehot[:, :, None] * inner[:, None, :]).reshape(-1, N_EXPERTS).astype(x.dtype)
    hg = jnp.einsum('td,edf->tef', xt, w_gate)
    hu = jnp.einsum('td,edf->tef', xt, w_up)
    hidden = jax.nn.silu(hg) * hu * combine[..., None]
    y = jnp.einsum('tef,efd->td', hidden, w_down)
    return y.reshape(bsz, t, d)


def finish_layer(h, o_attn, y_conv, p, g_o, w_out, g_ffn, w_grp, b_grp, w_rt, b_rt,
                 w_gate, w_up, w_down, g_ple_norm, w_ple_proj, w_ple_gate):
    bsz, t = h.shape[0], h.shape[1]
    o = rms_norm(o_attn, g_o).reshape(bsz, t, ATTN_WIDTH)
    h = h + jnp.concatenate([o, y_conv], axis=-1) @ w_out
    h = h + hier_moe(rms_norm(h, g_ffn), w_grp, b_grp, w_rt, b_rt, w_gate, w_up, w_down)
    gate = jax.nn.sigmoid(rms_norm(h, g_ple_norm) @ w_ple_gate)
    return h + gate * (p @ w_ple_proj)


def setup_inputs(seed: int = 0) -> dict:
    key = jax.random.key(seed)
    ks = jax.random.split(key, 40)
    f32 = jnp.float32
    n_pages = PAST_LEN // PAGE_SIZE
    n_used = DEC_BATCH * n_pages
    n_pool = n_used + max(1, n_used // 4)
    nrm = lambda k, shape, scale: jax.random.normal(k, shape, f32) * scale
    gain = lambda k, shape: 1.0 + 0.02 * jax.random.normal(k, shape, f32)
    page_table = jax.random.permutation(ks[3], n_pool)[:n_used].reshape(DEC_BATCH, n_pages).astype(jnp.int32)
    return {
        'x_prompt': nrm(ks[0], (BATCH, SEQ, D_MODEL), 1.0),
        'x_sample': nrm(ks[1], (DEC_BATCH, DEC_SEQ, D_MODEL), 1.0),
        'cache_k': nrm(ks[2], (DEPTH, n_pool, PAGE_SIZE, N_HEADS, HEAD_DIM), 1.0),
        'cache_v': nrm(ks[4], (DEPTH, n_pool, PAGE_SIZE, N_HEADS, HEAD_DIM), 1.0),
        'state_conv': nrm(ks[5], (DEPTH, DEC_BATCH, CONV_KERNEL - 1, CONV_CH), 0.5),
        'page_table': page_table,
        'p_prompt': nrm(ks[6], (DEPTH, BATCH, SEQ, PLE_DIM), 1.0),
        'p_sample': nrm(ks[7], (DEPTH, DEC_BATCH, DEC_SEQ, PLE_DIM), 1.0),
        'g_mix_norm': gain(ks[8], (DEPTH, D_MODEL)),
        'w_in': nrm(ks[9], (DEPTH, D_MODEL, IN_COLS), D_MODEL ** -0.5),
        'g_q': gain(ks[10], (DEPTH, HEAD_DIM)),
        'g_k': gain(ks[11], (DEPTH, HEAD_DIM)),
        'sb_bias': SB_BIAS_INIT + 0.5 * jax.random.normal(ks[30], (DEPTH, N_HEADS), f32),
        'g_o': gain(ks[12], (DEPTH, N_HEADS, HEAD_DIM)),
        'w_dw': nrm(ks[13], (DEPTH, CONV_KERNEL, CONV_CH), CONV_KERNEL ** -0.5),
        'b_dw': nrm(ks[14], (DEPTH, CONV_CH), 0.01),
        'ln_g': gain(ks[15], (DEPTH, CONV_CH)),
        'ln_b': nrm(ks[16], (DEPTH, CONV_CH), 0.01),
        'beta_conv': gain(ks[17], (DEPTH, CONV_CH)),
        'w_out': nrm(ks[18], (DEPTH, MIX_WIDTH, D_MODEL), MIX_WIDTH ** -0.5),
        'g_ffn_norm': gain(ks[19], (DEPTH, D_MODEL)),
        'w_grp': nrm(ks[20], (DEPTH, D_MODEL, N_GROUPS), D_MODEL ** -0.5),
        'b_grp': nrm(ks[21], (DEPTH, N_GROUPS), 0.01),
        'w_rt': nrm(ks[22], (DEPTH, D_MODEL, N_EXPERTS), D_MODEL ** -0.5),
        'b_rt': nrm(ks[23], (DEPTH, N_EXPERTS), 0.01),
        'w_gate': nrm(ks[24], (DEPTH, N_EXPERTS, D_MODEL, D_EXPERT), D_MODEL ** -0.5),
        'w_up': nrm(ks[25], (DEPTH, N_EXPERTS, D_MODEL, D_EXPERT), D_MODEL ** -0.5),
        'w_down': nrm(ks[26], (DEPTH, N_EXPERTS, D_EXPERT, D_MODEL), D_EXPERT ** -0.5),
        'g_ple_norm': gain(ks[27], (DEPTH, D_MODEL)),
        'w_ple_proj': nrm(ks[28], (DEPTH, PLE_DIM, D_MODEL), PLE_DIM ** -0.5),
        'w_ple_gate': nrm(ks[29], (DEPTH, D_MODEL, D_MODEL), D_MODEL ** -0.5),
    }


def reference(x_prompt, x_sample, cache_k, cache_v, state_conv, page_table, p_prompt, p_sample,
              g_mix_norm, w_in, g_q, g_k, sb_bias, g_o, w_dw, b_dw, ln_g, ln_b, beta_conv, w_out,
              g_ffn_norm, w_grp, b_grp, w_rt, b_rt, w_gate, w_up, w_down,
              g_ple_norm, w_ple_proj, w_ple_gate):
    hp, hs = x_prompt, x_sample
    kp_l, vp_l, cp_l, ks_l, vs_l, cs_l = [], [], [], [], [], []
    for i in range(DEPTH):
        qp, kp, vp, up = mix_inputs(hp, g_mix_norm[i], w_in[i], g_q[i], g_k[i])
        op = stick_breaking_prompt(qp, kp, vp, sb_bias[i])
        hist0 = jnp.zeros((hp.shape[0], CONV_KERNEL - 1, CONV_CH), up.dtype)
        yp, cp = conformer_conv(up, hist0, w_dw[i], b_dw[i], ln_g[i], ln_b[i], beta_conv[i])
        hp = finish_layer(hp, op, yp, p_prompt[i], g_o[i], w_out[i], g_ffn_norm[i], w_grp[i], b_grp[i],
                          w_rt[i], b_rt[i], w_gate[i], w_up[i], w_down[i], g_ple_norm[i],
                          w_ple_proj[i], w_ple_gate[i])
        qs, ks_, vs, us = mix_inputs(hs, g_mix_norm[i], w_in[i], g_q[i], g_k[i])
        os_ = stick_breaking_sample(qs, ks_, vs, cache_k, cache_v, i, page_table, sb_bias[i])
        ys, cs = conformer_conv(us, state_conv[i], w_dw[i], b_dw[i], ln_g[i], ln_b[i], beta_conv[i])
        hs = finish_layer(hs, os_, ys, p_sample[i], g_o[i], w_out[i], g_ffn_norm[i], w_grp[i], b_grp[i],
                          w_rt[i], b_rt[i], w_gate[i], w_up[i], w_down[i], g_ple_norm[i],
                          w_ple_proj[i], w_ple_gate[i])
        kp_l.append(kp); vp_l.append(vp); cp_l.append(cp)
        ks_l.append(ks_); vs_l.append(vs); cs_l.append(cs)
    k_prompt = jnp.stack(kp_l, 0)
    v_prompt = jnp.stack(vp_l, 0)
    conv_prompt = jnp.stack(cp_l, 0)
    k_sample = jnp.stack(ks_l, 0)
    v_sample = jnp.stack(vs_l, 0)
    conv_sample = jnp.stack(cs_l, 0)
    return (hp, hs, k_prompt, v_prompt, conv_prompt, k_sample, v_sample, conv_sample)
```

```python
import functools

import jax
import jax.numpy as jnp
from jax import lax
from jax.experimental import pallas as pl
from jax.experimental.pallas import tpu as pltpu

F32 = jnp.float32
BF16 = jnp.bfloat16
I32 = jnp.int32

D_MODEL = 2048
HEAD_DIM = 128
N_HEADS = 8
ATTN_WIDTH = N_HEADS * HEAD_DIM
CONV_CH = D_MODEL - ATTN_WIDTH
CONV_KERNEL = 31
HIST = CONV_KERNEL - 1
N_GROUPS = 4
EXPERTS_PER_GROUP = 8
N_EXPERTS = N_GROUPS * EXPERTS_PER_GROUP
D_EXPERT = D_MODEL // 4
PLE_DIM = 256
PAGE_SIZE = 128
EPS = 1e-6

LANES = 128
VMEM_LIMIT = 56 * 1024 * 1024

MIX_TM = 768
MIX_TN = 512
ATT_T = 256
CONV_TT = 256
HALO = 32
TOK_TM = 256
ROW_TM = 256
PAGE_BUFS = 3


def _cparams(sem):
    return pltpu.CompilerParams(dimension_semantics=sem, vmem_limit_bytes=VMEM_LIMIT)


def _neg_softplus(z):
    return -(jnp.maximum(z, 0.0) + jnp.log1p(jnp.exp(-jnp.abs(z))))


def _split_dot(l, m):
    hi = l.astype(BF16)
    lo = (l - hi.astype(F32)).astype(BF16)
    return (jnp.dot(hi, m, preferred_element_type=F32)
            + jnp.dot(lo, m, preferred_element_type=F32))


def _mix_kernel(x_ref, g_ref, wa_ref, wb_ref, gq_ref, gk_ref,
                q_ref, k_ref, v_ref, kb_ref, vb_ref, u_ref, xn_ref):
    j = pl.program_id(1)

    @pl.when(j == 0)
    def _():
        x = x_ref[...]
        ms = jnp.mean(x * x, axis=-1, keepdims=True)
        xn_ref[...] = (x * lax.rsqrt(ms + EPS) * g_ref[...]).astype(BF16)

    a = xn_ref[...]
    z = jnp.dot(a, wa_ref[...], preferred_element_type=F32)

    def head_norm(g):
        outs = []
        for hh in range(MIX_TN // HEAD_DIM):
            zh = z[:, hh * HEAD_DIM:(hh + 1) * HEAD_DIM]
            ms = jnp.mean(zh * zh, axis=-1, keepdims=True)
            outs.append(zh * lax.rsqrt(ms + EPS) * g)
        return jnp.concatenate(outs, axis=-1)

    @pl.when(j < 2)
    def _():
        q_ref[...] = (head_norm(gq_ref[...]) * (HEAD_DIM ** -0.5)).astype(BF16)

    @pl.when((j >= 2) & (j < 4))
    def _():
        kn = head_norm(gk_ref[...])
        k_ref[...] = kn
        kb_ref[...] = kn.astype(BF16)

    @pl.when((j >= 4) & (j < 6))
    def _():
        v_ref[...] = z
        vb_ref[...] = z.astype(BF16)

    @pl.when(j >= 6)
    def _():
        zb = jnp.dot(a, wb_ref[...], preferred_element_type=F32)
        u_ref[...] = z * jax.nn.sigmoid(zb)


def _mix_in(x_all, g_norm, w_in_bf, g_q, g_k):
    t = x_all.shape[0]
    nt = t // MIX_TM
    half = ATTN_WIDTH // MIX_TN

    def part_map(p):
        return lambda i, j: (i, jnp.clip(j - p * half, 0, half - 1))

    f32_out = jax.ShapeDtypeStruct((t, ATTN_WIDTH), F32)
    bf_out = jax.ShapeDtypeStruct((t, ATTN_WIDTH), BF16)
    out_block = lambda p: pl.BlockSpec((MIX_TM, MIX_TN), part_map(p))
    return pl.pallas_call(
        _mix_kernel,
        grid=(nt, 4 * half),
        in_specs=[
            pl.BlockSpec((MIX_TM, D_MODEL), lambda i, j: (i, 0)),
            pl.BlockSpec((1, D_MODEL), lambda i, j: (0, 0)),
            pl.BlockSpec((D_MODEL, MIX_TN), lambda i, j: (0, j)),
            pl.BlockSpec((D_MODEL, MIX_TN), lambda i, j: (0, jnp.maximum(j, 3 * half) + half)),
            pl.BlockSpec((1, HEAD_DIM), lambda i, j: (0, 0)),
            pl.BlockSpec((1, HEAD_DIM), lambda i, j: (0, 0)),
        ],
        out_specs=[out_block(0), out_block(1), out_block(2), out_block(1), out_block(2), out_block(3)],
        out_shape=[bf_out, f32_out, f32_out, bf_out, bf_out, f32_out],
        scratch_shapes=[pltpu.VMEM((MIX_TM, D_MODEL), BF16)],
        compiler_params=_cparams(("arbitrary", "arbitrary")),
        name="mix_in",
    )(x_all, g_norm, w_in_bf, w_in_bf, g_q, g_k)


def _attn_prompt_kernel(bias_ref, q_ref, k_ref, v_ref, go_ref, o_ref):
    h = pl.program_id(1)
    i = pl.program_id(2)
    bias = bias_ref[h]
    q = q_ref[...]
    row = lax.broadcasted_iota(I32, (ATT_T, ATT_T), 0)
    col = lax.broadcasted_iota(I32, (ATT_T, ATT_T), 1)
    later = (row > col).astype(BF16)
    causal = col < row

    def block(kb, carry, masked):
        o_acc, c = carry
        ks = pl.multiple_of(kb * ATT_T, ATT_T)
        k = k_ref[pl.ds(ks, ATT_T), :]
        v = v_ref[pl.ds(ks, ATT_T), :]
        z = lax.dot_general(q, k, (((1,), (1,)), ((), ())), preferred_element_type=F32) + bias
        l = _neg_softplus(z)
        lm = jnp.where(causal, l, 0.0) if masked else l
        r = _split_dot(lm, later)
        a = jnp.exp(z + l + r)
        if masked:
            a = jnp.where(causal, a, 0.0)
        o_acc = o_acc + jnp.exp(c) * jnp.dot(a.astype(BF16), v, preferred_element_type=F32)
        c = c + r[:, 0:1] + lm[:, 0:1]
        return o_acc, c

    carry = (jnp.zeros((ATT_T, HEAD_DIM), F32), jnp.zeros((ATT_T, 1), F32))
    carry = block(i, carry, True)
    o_acc, _ = lax.fori_loop(0, i, lambda t, cr: block(i - 1 - t, cr, False), carry)
    ms = jnp.mean(o_acc * o_acc, axis=-1, keepdims=True)
    o_ref[...] = (o_acc * lax.rsqrt(ms + EPS) * go_ref[...]).astype(BF16)


def _attn_prompt(sb_bias, q_bf, kb, vb, g_o_row, batch, seq):
    nq = seq // ATT_T
    return pl.pallas_call(
        _attn_prompt_kernel,
        grid=(batch, N_HEADS, nq),
        in_specs=[
            pl.BlockSpec(memory_space=pltpu.SMEM),
            pl.BlockSpec((ATT_T, HEAD_DIM), lambda b, h, i: (b * nq + i, h)),
            pl.BlockSpec((seq, HEAD_DIM), lambda b, h, i: (b, h)),
            pl.BlockSpec((seq, HEAD_DIM), lambda b, h, i: (b, h)),
            pl.BlockSpec((1, HEAD_DIM), lambda b, h, i: (0, h)),
        ],
        out_specs=pl.BlockSpec((ATT_T, HEAD_DIM), lambda b, h, i: (b * nq + i, h)),
        out_shape=jax.ShapeDtypeStruct((batch * seq, ATTN_WIDTH), BF16),
        compiler_params=_cparams(("arbitrary", "arbitrary", "arbitrary")),
        name="attn_prompt",
    )(sb_bias, q_bf, kb, vb, g_o_row)


def _attn_sample_kernel(pt_ref, q_ref, kn_ref, vn_ref, bias_ref, go_ref, ck_hbm, cv_hbm,
                        o_ref, kbuf, vbuf, sem, *, n_pages, dec_seq):
    b = pl.program_id(0)
    rows = N_HEADS * dec_seq

    def page_copies(pg, slot):
        page = pt_ref[b * n_pages + pg]
        return (pltpu.make_async_copy(ck_hbm.at[page], kbuf.at[slot], sem.at[0, slot]),
                pltpu.make_async_copy(cv_hbm.at[page], vbuf.at[slot], sem.at[1, slot]))

    def start(pg, slot):
        for cp in page_copies(pg, slot):
            cp.start()

    for d in range(PAGE_BUFS - 1):
        start(n_pages - 1 - d, d)

    q = q_ref[0]
    q_rep = jnp.concatenate([q] * N_HEADS, axis=0)
    r_head = lax.broadcasted_iota(I32, (rows, ATTN_WIDTH), 0) // dec_seq
    c_head = lax.broadcasted_iota(I32, (rows, ATTN_WIDTH), 1) // HEAD_DIM
    q_bd = jnp.where(r_head == c_head, q_rep, 0.0).astype(BF16)

    krow = lax.broadcasted_iota(I32, (PAGE_SIZE, PAGE_SIZE), 0)
    kcol = lax.broadcasted_iota(I32, (PAGE_SIZE, PAGE_SIZE), 1)
    later = (krow > kcol).astype(BF16)
    bias = bias_ref[...]

    def block(kf, vf, carry, mask):
        o_acc, c = carry
        z = lax.dot_general(q_bd, kf.astype(BF16), (((1,), (1,)), ((), ())),
                            preferred_element_type=F32) + bias
        l = _neg_softplus(z)
        lm = l if mask is None else jnp.where(mask, l, 0.0)
        r = _split_dot(lm, later)
        a = jnp.exp(z + l + r)
        if mask is not None:
            a = jnp.where(mask, a, 0.0)
        o_full = jnp.dot(a.astype(BF16), vf.astype(BF16), preferred_element_type=F32)
        o_diag = jnp.concatenate(
            [o_full[hh * dec_seq:(hh + 1) * dec_seq, hh * HEAD_DIM:(hh + 1) * HEAD_DIM]
             for hh in range(N_HEADS)], axis=0)
        o_acc = o_acc + jnp.exp(c) * o_diag
        c = c + r[:, 0:1] + lm[:, 0:1]
        return o_acc, c

    pad = jnp.zeros((PAGE_SIZE - dec_seq, ATTN_WIDTH), F32)
    k_new = jnp.concatenate([kn_ref[0], pad], axis=0)
    v_new = jnp.concatenate([vn_ref[0], pad], axis=0)
    qi = lax.broadcasted_iota(I32, (rows, PAGE_SIZE), 0) % dec_seq
    kj = lax.broadcasted_iota(I32, (rows, PAGE_SIZE), 1)
    carry = (jnp.zeros((rows, HEAD_DIM), F32), jnp.zeros((rows, 1), F32))
    carry = block(k_new, v_new, carry, kj < qi)

    def page_step(t, carry):
        slot = t % PAGE_BUFS
        for cp in page_copies(0, slot):
            cp.wait()
        nxt = t + PAGE_BUFS - 1

        @pl.when(nxt < n_pages)
        def _():
            start(n_pages - 1 - nxt, nxt % PAGE_BUFS)

        return block(kbuf[slot], vbuf[slot], carry, None)

    o_acc, _ = lax.fori_loop(0, n_pages, page_step, carry)
    ms = jnp.mean(o_acc * o_acc, axis=-1, keepdims=True)
    o_ref[0] = o_acc * lax.rsqrt(ms + EPS) * go_ref[...]


def _attn_sample(page_table, q_s, k_s, v_s, bias_rows, go_rows, cache_k, cache_v):
    nb, dec_seq, _ = q_s.shape
    n_pages = page_table.shape[1]
    rows = N_HEADS * dec_seq
    seq_block = pl.BlockSpec((1, dec_seq, ATTN_WIDTH), lambda b, pt: (b, 0, 0))
    grid_spec = pltpu.PrefetchScalarGridSpec(
        num_scalar_prefetch=1,
        grid=(nb,),
        in_specs=[
            seq_block, seq_block, seq_block,
            pl.BlockSpec((rows, PAGE_SIZE), lambda b, pt: (0, 0)),
            pl.BlockSpec((rows, HEAD_DIM), lambda b, pt: (0, 0)),
            pl.BlockSpec(memory_space=pl.ANY),
            pl.BlockSpec(memory_space=pl.ANY),
        ],
        out_specs=pl.BlockSpec((1, rows, HEAD_DIM), lambda b, pt: (b, 0, 0)),
        scratch_shapes=[
            pltpu.VMEM((PAGE_BUFS, PAGE_SIZE, ATTN_WIDTH), F32),
            pltpu.VMEM((PAGE_BUFS, PAGE_SIZE, ATTN_WIDTH), F32),
            pltpu.SemaphoreType.DMA((2, PAGE_BUFS)),
        ],
    )
    return pl.pallas_call(
        functools.partial(_attn_sample_kernel, n_pages=n_pages, dec_seq=dec_seq),
        grid_spec=grid_spec,
        out_shape=jax.ShapeDtypeStruct((nb, rows, HEAD_DIM), F32),
        compiler_params=_cparams(("arbitrary",)),
        name="attn_sample",
    )(page_table.reshape(-1), q_s, k_s, v_s, bias_rows, go_rows, cache_k, cache_v)


def _conv_kernel(u_ref, hist_ref, w_ref, b_ref, g_ref, lb_ref, beta_ref, y_ref, win_ref, acc_ref, *, tt):
    i = pl.program_id(1)

    @pl.when(i == 0)
    def _():
        win_ref[0:HALO, :] = hist_ref[0]

    @pl.when(i > 0)
    def _():
        win_ref[0:HALO, :] = win_ref[tt:tt + HALO, :]

    win_ref[HALO:HALO + tt, :] = u_ref[0]
    off = HALO - HIST
    for cb in range(CONV_CH // LANES):
        cs = slice(cb * LANES, (cb + 1) * LANES)
        acc = jnp.zeros((tt, LANES), F32)
        for k in range(CONV_KERNEL):
            acc = acc + w_ref[k:k + 1, cs] * win_ref[off + k:off + k + tt, cs]
        acc_ref[:, cs] = acc
    y = acc_ref[...] + b_ref[...]
    mu = jnp.mean(y, axis=-1, keepdims=True)
    yc = y - mu
    var = jnp.mean(yc * yc, axis=-1, keepdims=True)
    y = yc * lax.rsqrt(var + EPS) * g_ref[...] + lb_ref[...]
    y_ref[0] = (y * jax.nn.sigmoid(y) * beta_ref[...]).astype(BF16)


def _conv(u, hist, w_dw, b_dw, ln_g, ln_b, beta, tt):
    nb, s, c = u.shape
    row = pl.BlockSpec((1, c), lambda b, i: (0, 0))
    return pl.pallas_call(
        functools.partial(_conv_kernel, tt=tt),
        grid=(nb, s // tt),
        in_specs=[
            pl.BlockSpec((1, tt, c), lambda b, i: (b, i, 0)),
            pl.BlockSpec((1, HALO, c), lambda b, i: (b, 0, 0)),
            pl.BlockSpec((CONV_KERNEL, c), lambda b, i: (0, 0)),
            row, row, row, row,
        ],
        out_specs=pl.BlockSpec((1, tt, c), lambda b, i: (b, i, 0)),
        out_shape=jax.ShapeDtypeStruct((nb, s, c), BF16),
        scratch_shapes=[pltpu.VMEM((HALO + tt, c), F32), pltpu.VMEM((tt, c), F32)],
        compiler_params=_cparams(("arbitrary", "arbitrary")),
        name="conv",
    )(u, hist, w_dw, b_dw, ln_g, ln_b, beta)


R_E1, R_E2, R_W1, R_W2, R_RANK1, R_RANK2 = range(6)
GRP_LANE0 = N_EXPERTS


def _lane_pick(x, lane, idx):
    return jnp.sum(jnp.where(lane == idx, x, 0.0), axis=-1, keepdims=True)


def _post_kernel(h_ref, o_ref, yc_ref, wt_ref, wb_ref, g_ref, wr_ref, br_ref,
                 h1_ref, xn_ref, route_ref, tot_ref, run_ref):
    i = pl.program_id(0)

    @pl.when(i == 0)
    def _():
        run_ref[...] = jnp.zeros_like(run_ref)

    h1 = (h_ref[...]
          + jnp.dot(o_ref[...], wt_ref[...], preferred_element_type=F32)
          + jnp.dot(yc_ref[...], wb_ref[...], preferred_element_type=F32))
    h1_ref[...] = h1
    ms = jnp.mean(h1 * h1, axis=-1, keepdims=True)
    xn = h1 * lax.rsqrt(ms + EPS) * g_ref[...]
    xn_ref[...] = xn

    logits = jnp.dot(xn, wr_ref[...], preferred_element_type=F32,
                     precision=lax.Precision.HIGHEST) + br_ref[...]
    tm = logits.shape[0]
    lane = lax.broadcasted_iota(I32, (tm, LANES), 1)
    neg = -jnp.inf
    is_grp = (lane >= GRP_LANE0) & (lane < GRP_LANE0 + N_GROUPS)
    glog = jnp.where(is_grp, logits, neg)
    gmax = jnp.max(glog, axis=-1, keepdims=True)
    gidx = jnp.min(jnp.where(glog == gmax, lane - GRP_LANE0, LANES), axis=-1, keepdims=True)
    g_w = 1.0 / jnp.sum(jnp.exp(glog - gmax), axis=-1, keepdims=True)

    in_grp = (lane < N_EXPERTS) & (lane // EXPERTS_PER_GROUP == gidx)
    elog = jnp.where(in_grp, logits, neg)
    v1 = jnp.max(elog, axis=-1, keepdims=True)
    i1 = jnp.min(jnp.where(elog == v1, lane, LANES), axis=-1, keepdims=True)
    elog2 = jnp.where(lane == i1, neg, elog)
    v2 = jnp.max(elog2, axis=-1, keepdims=True)
    i2 = jnp.min(jnp.where(elog2 == v2, lane, LANES), axis=-1, keepdims=True)
    ew = jnp.exp(v2 - v1)
    w1 = g_w / (1.0 + ew)
    w2 = g_w * ew / (1.0 + ew)

    onehot = ((lane == i1) | (lane == i2)).astype(F32)
    trow = lax.broadcasted_iota(I32, (tm, tm), 0)
    tcol = lax.broadcasted_iota(I32, (tm, tm), 1)
    earlier = (tcol < trow).astype(BF16)
    before = run_ref[0:1, :] + jnp.dot(earlier, onehot.astype(BF16), preferred_element_type=F32)
    rank1 = _lane_pick(before, lane, i1)
    rank2 = _lane_pick(before, lane, i2)
    run_ref[...] = run_ref[...] + jnp.sum(onehot, axis=0, keepdims=True)
    tot_ref[...] = run_ref[...]

    rec = jnp.zeros((tm, LANES), F32)
    for idx, val in ((R_E1, i1.astype(F32)), (R_E2, i2.astype(F32)), (R_W1, w1), (R_W2, w2),
                     (R_RANK1, rank1), (R_RANK2, rank2)):
        rec = jnp.where(lane == idx, val, rec)
    route_ref[...] = rec


def _post(h_all, o_all, yc_all, w_top, w_bot, g_ffn, w_router, b_router):
    t = h_all.shape[0]
    tok = lambda w: pl.BlockSpec((TOK_TM, w), lambda i: (i, 0))
    full = lambda a: pl.BlockSpec(a.shape, lambda i: (0,) * a.ndim)
    return pl.pallas_call(
        _post_kernel,
        grid=(t // TOK_TM,),
        in_specs=[tok(D_MODEL), tok(ATTN_WIDTH), tok(CONV_CH), full(w_top), full(w_bot),
                  full(g_ffn), full(w_router), full(b_router)],
        out_specs=[tok(D_MODEL), tok(D_MODEL), tok(LANES), pl.BlockSpec((8, LANES), lambda i: (0, 0))],
        out_shape=[jax.ShapeDtypeStruct((t, D_MODEL), F32), jax.ShapeDtypeStruct((t, D_MODEL), F32),
                   jax.ShapeDtypeStruct((t, LANES), F32), jax.ShapeDtypeStruct((8, LANES), F32)],
        scratch_shapes=[pltpu.VMEM((8, LANES), F32)],
        compiler_params=_cparams(("arbitrary",)),
        name="post",
    )(h_all, o_all, yc_all, w_top, w_bot, g_ffn, w_router, b_router)


M_TILE_EXPERT, M_NUM_TILES = 0, 1


def _positions_kernel(route_ref, tot_ref, pos_ref, meta_ref):
    t = route_ref.shape[0]
    lane1 = lax.broadcasted_iota(I32, (8, LANES), 1)
    total = tot_ref[...]
    ntile = jnp.where(lane1 < N_EXPERTS, jnp.ceil(total / ROW_TM), 0.0)
    srow = lax.broadcasted_iota(I32, (LANES, LANES), 0)
    scol = lax.broadcasted_iota(I32, (LANES, LANES), 1)
    start_tile = jnp.dot(ntile.astype(BF16), (srow < scol).astype(BF16), preferred_element_type=F32)
    start_row = start_tile[0:1, :] * ROW_TM

    route = route_ref[...]
    lane = lax.broadcasted_iota(I32, (t, LANES), 1)
    e1 = _lane_pick(route, lane, R_E1).astype(I32)
    e2 = _lane_pick(route, lane, R_E2).astype(I32)
    p1 = _lane_pick(jnp.broadcast_to(start_row, (t, LANES)), lane, e1) + _lane_pick(route, lane, R_RANK1)
    p2 = _lane_pick(jnp.broadcast_to(start_row, (t, LANES)), lane, e2) + _lane_pick(route, lane, R_RANK2)
    pos_ref[...] = jnp.where(lane == 0, p1, jnp.where(lane == 1, p2, 0.0)).astype(I32)

    end_tile = start_tile[0:1, :] + ntile[0:1, :]
    end_col = jnp.sum(jnp.where(srow == scol, jnp.broadcast_to(end_tile, (LANES, LANES)), 0.0),
                      axis=-1, keepdims=True)
    done = ((end_col <= scol.astype(F32)) & (srow < N_EXPERTS)).astype(F32)
    tile_expert = jnp.minimum(jnp.sum(done, axis=0, keepdims=True), N_EXPERTS - 1.0)
    num_tiles = jnp.sum(ntile[0:1, :], axis=-1, keepdims=True)
    row8 = lax.broadcasted_iota(I32, (8, LANES), 0)
    meta = jnp.where(row8 == M_TILE_EXPERT, jnp.broadcast_to(tile_expert, (8, LANES)),
                     jnp.where(row8 == M_NUM_TILES, jnp.broadcast_to(num_tiles, (8, LANES)), 0.0))
    meta_ref[...] = meta.astype(I32)


def _positions(route, totals):
    t = route.shape[0]
    return pl.pallas_call(
        _positions_kernel,
        out_shape=[jax.ShapeDtypeStruct((t, LANES), I32), jax.ShapeDtypeStruct((8, LANES), I32)],
        compiler_params=pltpu.CompilerParams(vmem_limit_bytes=VMEM_LIMIT),
        name="positions",
    )(route, totals)


def _experts_kernel(te_ref, nt_ref, p1_ref, p2_ref, xn_hbm, wg_ref, wu_ref, wd_ref,
                    y_ref, row_tok, xg, wg_bf, wu_bf, wd_bf, sem, *, n_tok, n_rows):
    j = pl.program_id(0)

    @pl.when(j == 0)
    def _():
        def clear(r, _):
            row_tok[r] = 0
            return 0
        lax.fori_loop(0, n_rows, clear, 0)

        def place(t, _):
            row_tok[p1_ref[t]] = t
            row_tok[p2_ref[t]] = t
            return 0
        lax.fori_loop(0, n_tok, place, 0)

    @pl.when(j < nt_ref[0])
    def _():
        def row_copy(r):
            tok = row_tok[j * ROW_TM + r]
            return pltpu.make_async_copy(xn_hbm.at[pl.ds(tok, 1), :], xg.at[pl.ds(r, 1), :], sem.at[0])

        def issue(r, _):
            row_copy(r).start()
            return 0
        lax.fori_loop(0, ROW_TM, issue, 0)

        new_expert = (j == 0) | (te_ref[j] != te_ref[jnp.maximum(j - 1, 0)])

        @pl.when(new_expert)
        def _():
            wg_bf[...] = wg_ref[0].astype(BF16)
            wu_bf[...] = wu_ref[0].astype(BF16)
            wd_bf[...] = wd_ref[0].astype(BF16)

        def drain(r, _):
            row_copy(r).wait()
            return 0
        lax.fori_loop(0, ROW_TM, drain, 0)

        x = xg[...].astype(BF16)
        hg = jnp.dot(x, wg_bf[...], preferred_element_type=F32)
        hu = jnp.dot(x, wu_bf[...], preferred_element_type=F32)
        hid = (hg * jax.nn.sigmoid(hg) * hu).astype(BF16)
        y_ref[...] = jnp.dot(hid, wd_bf[...], preferred_element_type=F32)

    @pl.when(j >= nt_ref[0])
    def _():
        y_ref[...] = jnp.zeros_like(y_ref)


def _experts(tile_expert, num_tiles, pos1, pos2, xn, w_gate, w_up, w_down):
    n_tok = xn.shape[0]
    max_tiles = (2 * n_tok + N_EXPERTS * (ROW_TM - 1)) // ROW_TM
    n_rows = max_tiles * ROW_TM
    w_in_spec = pl.BlockSpec((1, D_MODEL, D_EXPERT), lambda j, te, nt, p1, p2: (te[j], 0, 0))
    grid_spec = pltpu.PrefetchScalarGridSpec(
        num_scalar_prefetch=4,
        grid=(max_tiles,),
        in_specs=[
            pl.BlockSpec(memory_space=pl.ANY),
            w_in_spec, w_in_spec,
            pl.BlockSpec((1, D_EXPERT, D_MODEL), lambda j, te, nt, p1, p2: (te[j], 0, 0)),
        ],
        out_specs=pl.BlockSpec((ROW_TM, D_MODEL), lambda j, te, nt, p1, p2: (j, 0)),
        scratch_shapes=[
            pltpu.SMEM((n_rows,), I32),
            pltpu.VMEM((ROW_TM, D_MODEL), F32),
            pltpu.VMEM((D_MODEL, D_EXPERT), BF16),
            pltpu.VMEM((D_MODEL, D_EXPERT), BF16),
            pltpu.VMEM((D_EXPERT, D_MODEL), BF16),
            pltpu.SemaphoreType.DMA((1,)),
        ],
    )
    return pl.pallas_call(
        functools.partial(_experts_kernel, n_tok=n_tok, n_rows=n_rows),
        grid_spec=grid_spec,
        out_shape=jax.ShapeDtypeStruct((n_rows, D_MODEL), F32),
        compiler_params=_cparams(("arbitrary",)),
        name="experts",
    )(tile_expert, num_tiles, pos1, pos2, xn, w_gate, w_up, w_down)


def _final_kernel(p1_ref, p2_ref, h1_ref, route_ref, p_ref, ys_hbm, g_ref, wpg_ref, wpp_ref,
                  outp_ref, outs_ref, y1, y2, sem, *, n_prompt_tiles):
    i = pl.program_id(0)

    def row_copies(r):
        tok = i * TOK_TM + r
        return (pltpu.make_async_copy(ys_hbm.at[pl.ds(p1_ref[tok], 1), :], y1.at[pl.ds(r, 1), :], sem.at[0]),
                pltpu.make_async_copy(ys_hbm.at[pl.ds(p2_ref[tok], 1), :], y2.at[pl.ds(r, 1), :], sem.at[1]))

    def issue(r, _):
        for cp in row_copies(r):
            cp.start()
        return 0
    lax.fori_loop(0, TOK_TM, issue, 0)

    route = route_ref[...]
    lane = lax.broadcasted_iota(I32, route.shape, 1)
    w1 = _lane_pick(route, lane, R_W1)
    w2 = _lane_pick(route, lane, R_W2)
    emb = jnp.dot(p_ref[...].astype(BF16), wpp_ref[...], preferred_element_type=F32)

    def drain(r, _):
        for cp in row_copies(r):
            cp.wait()
        return 0
    lax.fori_loop(0, TOK_TM, drain, 0)

    h2 = h1_ref[...] + w1 * y1[...] + w2 * y2[...]
    ms = jnp.mean(h2 * h2, axis=-1, keepdims=True)
    hn = (h2 * lax.rsqrt(ms + EPS) * g_ref[...]).astype(BF16)
    gate = jax.nn.sigmoid(jnp.dot(hn, wpg_ref[...], preferred_element_type=F32))
    res = h2 + gate * emb

    @pl.when(i < n_prompt_tiles)
    def _():
        outp_ref[...] = res

    @pl.when(i >= n_prompt_tiles)
    def _():
        outs_ref[...] = res


def _final(pos1, pos2, h1, route, p_all, ys, g_ple, w_ple_gate_bf, w_ple_proj_bf, n_prompt):
    t = h1.shape[0]
    npt = n_prompt // TOK_TM
    nst = (t - n_prompt) // TOK_TM
    tok = lambda w: pl.BlockSpec((TOK_TM, w), lambda i, p1, p2: (i, 0))
    full = lambda a: pl.BlockSpec(a.shape, lambda i, p1, p2: (0,) * a.ndim)
    grid_spec = pltpu.PrefetchScalarGridSpec(
        num_scalar_prefetch=2,
        grid=(t // TOK_TM,),
        in_specs=[tok(D_MODEL), tok(LANES), tok(PLE_DIM), pl.BlockSpec(memory_space=pl.ANY),
                  full(g_ple), full(w_ple_gate_bf), full(w_ple_proj_bf)],
        out_specs=[
            pl.BlockSpec((TOK_TM, D_MODEL), lambda i, p1, p2: (jnp.minimum(i, npt - 1), 0)),
            pl.BlockSpec((TOK_TM, D_MODEL), lambda i, p1, p2: (jnp.clip(i - npt, 0, nst - 1), 0)),
        ],
        scratch_shapes=[pltpu.VMEM((TOK_TM, D_MODEL), F32), pltpu.VMEM((TOK_TM, D_MODEL), F32),
                        pltpu.SemaphoreType.DMA((2,))],
    )
    return pl.pallas_call(
        functools.partial(_final_kernel, n_prompt_tiles=npt),
        grid_spec=grid_spec,
        out_shape=[jax.ShapeDtypeStruct((n_prompt, D_MODEL), F32),
                   jax.ShapeDtypeStruct((t - n_prompt, D_MODEL), F32)],
        compiler_params=_cparams(("arbitrary",)),
        name="final",
    )(pos1, pos2, h1, route, p_all, ys, g_ple, w_ple_gate_bf, w_ple_proj_bf)


def _layer(i, hp, hs, cache_k, cache_v, state_conv, page_table, p_prompt, p_sample,
           g_mix_norm, w_in, g_q, g_k, sb_bias, g_o, w_dw, b_dw, ln_g, ln_b, beta_conv, w_out,
           g_ffn_norm, w_grp, b_grp, w_rt, b_rt, w_gate, w_up, w_down,
           g_ple_norm, w_ple_proj, w_ple_gate):
    batch, seq, _ = hp.shape
    nb, dec_seq, _ = hs.shape
    n_prompt = batch * seq
    n_sample = nb * dec_seq
    row = lambda a: a.reshape(1, -1)

    x_all = jnp.concatenate([hp.reshape(n_prompt, D_MODEL), hs.reshape(n_sample, D_MODEL)], axis=0)
    q_bf, k_all, v_all, kb, vb, u_all = _mix_in(
        x_all, row(g_mix_norm[i]), w_in[i].astype(BF16), row(g_q[i]), row(g_k[i]))

    o_p = _attn_prompt(sb_bias[i], q_bf, kb, vb, row(g_o[i]), batch, seq)

    rows = N_HEADS * dec_seq
    split = lambda a: a[n_prompt:].astype(F32).reshape(nb, dec_seq, ATTN_WIDTH)
    bias_rows = jnp.broadcast_to(jnp.repeat(sb_bias[i], dec_seq)[:, None], (rows, PAGE_SIZE))
    go_rows = jnp.repeat(g_o[i], dec_seq, axis=0)
    n_pool = cache_k.shape[1]
    o_s = _attn_sample(page_table, split(q_bf) , split(k_all), split(v_all), bias_rows, go_rows,
                       cache_k[i].reshape(n_pool, PAGE_SIZE, ATTN_WIDTH),
                       cache_v[i].reshape(n_pool, PAGE_SIZE, ATTN_WIDTH))
    o_s = o_s.reshape(nb, N_HEADS, dec_seq, HEAD_DIM).transpose(0, 2, 1, 3).reshape(n_sample, ATTN_WIDTH)

    conv_args = (w_dw[i], row(b_dw[i]), row(ln_g[i]), row(ln_b[i]), row(beta_conv[i]))
    u_p = u_all[:n_prompt].reshape(batch, seq, CONV_CH)
    u_s = u_all[n_prompt:].reshape(nb, dec_seq, CONV_CH)
    hist_p = jnp.zeros((batch, HALO, CONV_CH), F32)
    hist_s = jnp.pad(state_conv[i], ((0, 0), (HALO - HIST, 0), (0, 0)))
    yc_p = _conv(u_p, hist_p, *conv_args, tt=CONV_TT)
    yc_s = _conv(u_s, hist_s, *conv_args, tt=dec_seq)

    o_all = jnp.concatenate([o_p, o_s.astype(BF16)], axis=0)
    yc_all = jnp.concatenate([yc_p.reshape(n_prompt, CONV_CH), yc_s.reshape(n_sample, CONV_CH)], axis=0)
    w_out_bf = w_out[i].astype(BF16)
    w_router = jnp.zeros((D_MODEL, LANES), F32)
    w_router = w_router.at[:, :N_EXPERTS].set(w_rt[i]).at[:, GRP_LANE0:GRP_LANE0 + N_GROUPS].set(w_grp[i])
    b_router = jnp.zeros((1, LANES), F32)
    b_router = b_router.at[0, :N_EXPERTS].set(b_rt[i]).at[0, GRP_LANE0:GRP_LANE0 + N_GROUPS].set(b_grp[i])
    h1, xn, route, totals = _post(x_all, o_all, yc_all, w_out_bf[:ATTN_WIDTH], w_out_bf[ATTN_WIDTH:],
                                  row(g_ffn_norm[i]), w_router, b_router)

    pos, meta = _positions(route, totals)
    pos1, pos2 = pos[:, 0], pos[:, 1]
    ys = _experts(meta[M_TILE_EXPERT], meta[M_NUM_TILES, :1], pos1, pos2, xn, w_gate[i], w_up[i], w_down[i])

    p_all = jnp.concatenate([p_prompt[i].reshape(n_prompt, PLE_DIM), p_sample[i].reshape(n_sample, PLE_DIM)], axis=0)
    y_p, y_s = _final(pos1, pos2, h1, route, p_all, ys, row(g_ple_norm[i]),
                      w_ple_gate[i].astype(BF16), w_ple_proj[i].astype(BF16), n_prompt)

    heads = lambda a, b_, s_: a.reshape(b_, s_, N_HEADS, HEAD_DIM)
    conv_p = u_p[:, seq - HIST:]
    conv_s = jnp.concatenate([state_conv[i], u_s], axis=1)[:, -HIST:]
    return (y_p.reshape(batch, seq, D_MODEL), y_s.reshape(nb, dec_seq, D_MODEL),
            heads(k_all[:n_prompt], batch, seq), heads(v_all[:n_prompt], batch, seq), conv_p,
            heads(k_all[n_prompt:], nb, dec_seq), heads(v_all[n_prompt:], nb, dec_seq), conv_s)


def kernel(x_prompt, x_sample, cache_k, cache_v, state_conv, page_table, p_prompt, p_sample, g_mix_norm, w_in, g_q, g_k, sb_bias, g_o, w_dw, b_dw, ln_g, ln_b, beta_conv, w_out, g_ffn_norm, w_grp, b_grp, w_rt, b_rt, w_gate, w_up, w_down, g_ple_norm, w_ple_proj, w_ple_gate):
    depth = w_in.shape[0]
    hp, hs = x_prompt, x_sample
    outs = []
    for i in range(depth):
        res = _layer(i, hp, hs, cache_k, cache_v, state_conv, page_table, p_prompt, p_sample,
                     g_mix_norm, w_in, g_q, g_k, sb_bias, g_o, w_dw, b_dw, ln_g, ln_b, beta_conv, w_out,
                     g_ffn_norm, w_grp, b_grp, w_rt, b_rt, w_gate, w_up, w_down,
                     g_ple_norm, w_ple_proj, w_ple_gate)
        hp, hs = res[0], res[1]
        outs.append(res[2:])
    stacked = tuple(jnp.stack([o[n] for o in outs], axis=0) for n in range(6))
    return (hp, hs) + stacked
```

```python
import functools

import jax
import jax.numpy as jnp
from jax import lax
from jax.experimental import pallas as pl
from jax.experimental.pallas import tpu as pltpu

F32 = jnp.float32
BF16 = jnp.bfloat16
I32 = jnp.int32

D_MODEL = 2048
HEAD_DIM = 128
N_HEADS = 8
ATTN_WIDTH = N_HEADS * HEAD_DIM
CONV_CH = D_MODEL - ATTN_WIDTH
CONV_KERNEL = 31
HIST = CONV_KERNEL - 1
N_GROUPS = 4
EXPERTS_PER_GROUP = 8
N_EXPERTS = N_GROUPS * EXPERTS_PER_GROUP
D_EXPERT = D_MODEL // 4
PLE_DIM = 256
PAGE_SIZE = 128
EPS = 1e-6

LANES = 128
VMEM_LIMIT = 56 * 1024 * 1024

MIX_TM = 768
MIX_TN = 512
ATT_T = 256
ATT_HEADS = 8
PAGE_GROUP = 4
CONV_TT = 256
HALO = 32
TOK_TM = 256
ROW_TM = 256
DMA_UNROLL = 8


def _cparams(sem):
    return pltpu.CompilerParams(dimension_semantics=sem, vmem_limit_bytes=VMEM_LIMIT)


def _log_survival(z):
    nz = -z
    return jnp.minimum(nz, 0.0) - jnp.log(1.0 + jnp.exp(jnp.minimum(z, nz)))


def _later_matrix(n):
    row = lax.broadcasted_iota(I32, (n, n), 0)
    col = lax.broadcasted_iota(I32, (n, n), 1)
    return (row > col).astype(BF16)


def _split_dot(l, m):
    hi = l.astype(BF16)
    lo = (l - hi.astype(F32)).astype(BF16)
    return (jnp.dot(hi, m, preferred_element_type=F32)
            + jnp.dot(lo, m, preferred_element_type=F32))


def _mix_kernel(x_ref, g_ref, wa_ref, wb_ref, gq_ref, gk_ref,
                q_ref, k_ref, v_ref, kb_ref, vb_ref, u_ref, xn_ref):
    j = pl.program_id(1)

    @pl.when(j == 0)
    def _():
        x = x_ref[...]
        ms = jnp.mean(x * x, axis=-1, keepdims=True)
        xn_ref[...] = (x * lax.rsqrt(ms + EPS) * g_ref[...]).astype(BF16)

    a = xn_ref[...]
    z = jnp.dot(a, wa_ref[...], preferred_element_type=F32)

    def head_norm(g):
        outs = []
        for hh in range(MIX_TN // HEAD_DIM):
            zh = z[:, hh * HEAD_DIM:(hh + 1) * HEAD_DIM]
            ms = jnp.mean(zh * zh, axis=-1, keepdims=True)
            outs.append(zh * lax.rsqrt(ms + EPS) * g)
        return jnp.concatenate(outs, axis=-1)

    @pl.when(j < 2)
    def _():
        q_ref[...] = (head_norm(gq_ref[...]) * (HEAD_DIM ** -0.5)).astype(BF16)

    @pl.when((j >= 2) & (j < 4))
    def _():
        kn = head_norm(gk_ref[...])
        k_ref[...] = kn
        kb_ref[...] = kn.astype(BF16)

    @pl.when((j >= 4) & (j < 6))
    def _():
        v_ref[...] = z
        vb_ref[...] = z.astype(BF16)

    @pl.when(j >= 6)
    def _():
        zb = jnp.dot(a, wb_ref[...], preferred_element_type=F32)
        u_ref[...] = z * jax.nn.sigmoid(zb)


def _mix_in(x_all, g_norm, w_in_bf, g_q, g_k):
    t = x_all.shape[0]
    nt = t // MIX_TM
    half = ATTN_WIDTH // MIX_TN

    def part_map(p):
        return lambda i, j: (i, jnp.clip(j - p * half, 0, half - 1))

    f32_out = jax.ShapeDtypeStruct((t, ATTN_WIDTH), F32)
    bf_out = jax.ShapeDtypeStruct((t, ATTN_WIDTH), BF16)
    out_block = lambda p: pl.BlockSpec((MIX_TM, MIX_TN), part_map(p))
    return pl.pallas_call(
        _mix_kernel,
        grid=(nt, 4 * half),
        in_specs=[
            pl.BlockSpec((MIX_TM, D_MODEL), lambda i, j: (i, 0)),
            pl.BlockSpec((1, D_MODEL), lambda i, j: (0, 0)),
            pl.BlockSpec((D_MODEL, MIX_TN), lambda i, j: (0, j)),
            pl.BlockSpec((D_MODEL, MIX_TN), lambda i, j: (0, jnp.maximum(j, 3 * half) + half)),
            pl.BlockSpec((1, HEAD_DIM), lambda i, j: (0, 0)),
            pl.BlockSpec((1, HEAD_DIM), lambda i, j: (0, 0)),
        ],
        out_specs=[out_block(0), out_block(1), out_block(2), out_block(1), out_block(2), out_block(3)],
        out_shape=[bf_out, f32_out, f32_out, bf_out, bf_out, f32_out],
        scratch_shapes=[pltpu.VMEM((MIX_TM, D_MODEL), BF16)],
        compiler_params=_cparams(("arbitrary", "arbitrary")),
        name="mix_in",
    )(x_all, g_norm, w_in_bf, w_in_bf, g_q, g_k)


def _attn_prompt_kernel(bias_ref, q_ref, k_ref, v_ref, go_ref, o_ref):
    hg = pl.program_id(1)
    i = pl.program_id(2)
    later = _later_matrix(ATT_T)
    heads = range(ATT_HEADS)
    hs = [slice(n * HEAD_DIM, (n + 1) * HEAD_DIM) for n in heads]
    bias = [bias_ref[hg * ATT_HEADS + n] for n in heads]
    qs = [q_ref[:, hs[n]] for n in heads]
    causal = (lax.broadcasted_iota(I32, (ATT_T, ATT_T), 1) < lax.broadcasted_iota(I32, (ATT_T, ATT_T), 0))

    def block(kb, carry, masked):
        ks = pl.multiple_of(kb * ATT_T, ATT_T)
        zs = [lax.dot_general(qs[n], k_ref[pl.ds(ks, ATT_T), hs[n]], (((1,), (1,)), ((), ())),
                              preferred_element_type=F32) + bias[n] for n in heads]
        ls = [_log_survival(z) for z in zs]
        lms = [jnp.where(causal, l, 0.0) for l in ls] if masked else ls
        rs = [_split_dot(lm, later) for lm in lms]
        avs = []
        for n in heads:
            a = jnp.exp(zs[n] + ls[n] + rs[n])
            if masked:
                a = jnp.where(causal, a, 0.0)
            avs.append(jnp.dot(a.astype(BF16), v_ref[pl.ds(ks, ATT_T), hs[n]], preferred_element_type=F32))
        return tuple((carry[n][0] + jnp.exp(carry[n][1]) * avs[n],
                      carry[n][1] + rs[n][:, 0:1] + lms[n][:, 0:1]) for n in heads)

    carry = tuple((jnp.zeros((ATT_T, HEAD_DIM), F32), jnp.zeros((ATT_T, 1), F32)) for _ in heads)
    carry = block(i, carry, True)
    carry = lax.fori_loop(0, i, lambda t, cr: block(i - 1 - t, cr, False), carry)
    for n in heads:
        o_acc = carry[n][0]
        ms = jnp.mean(o_acc * o_acc, axis=-1, keepdims=True)
        o_ref[:, hs[n]] = (o_acc * lax.rsqrt(ms + EPS) * go_ref[:, hs[n]]).astype(BF16)


def _attn_prompt(sb_bias, q_bf, kb, vb, g_o_row, batch, seq):
    nq = seq // ATT_T
    width = ATT_HEADS * HEAD_DIM
    return pl.pallas_call(
        _attn_prompt_kernel,
        grid=(batch, N_HEADS // ATT_HEADS, nq),
        in_specs=[
            pl.BlockSpec(memory_space=pltpu.SMEM),
            pl.BlockSpec((ATT_T, width), lambda b, h, i: (b * nq + i, h)),
            pl.BlockSpec((seq, width), lambda b, h, i: (b, h)),
            pl.BlockSpec((seq, width), lambda b, h, i: (b, h)),
            pl.BlockSpec((1, width), lambda b, h, i: (0, h)),
        ],
        out_specs=pl.BlockSpec((ATT_T, width), lambda b, h, i: (b * nq + i, h)),
        out_shape=jax.ShapeDtypeStruct((batch * seq, ATTN_WIDTH), BF16),
        compiler_params=_cparams(("arbitrary", "arbitrary", "arbitrary")),
        name="attn_prompt",
    )(sb_bias, q_bf, kb, vb, g_o_row)


def _attn_sample_kernel(pt_ref, q_ref, kn_ref, vn_ref, bias_ref, go_ref, ck_hbm, cv_hbm,
                        o_ref, kbuf, vbuf, sem, *, n_pages, dec_seq, n_seq):
    b = pl.program_id(0)
    rows = N_HEADS * dec_seq
    n_groups = n_pages // PAGE_GROUP

    def group_copies(seq, grp, slot):
        cps = []
        for g in range(PAGE_GROUP):
            page = pt_ref[seq * n_pages + n_pages - 1 - (grp * PAGE_GROUP + g)]
            cps.append(pltpu.make_async_copy(ck_hbm.at[page], kbuf.at[slot, g], sem.at[0, slot, g]))
            cps.append(pltpu.make_async_copy(cv_hbm.at[page], vbuf.at[slot, g], sem.at[1, slot, g]))
        return cps

    @pl.when(b == 0)
    def _():
        for cp in group_copies(0, 0, 0):
            cp.start()

    q = q_ref[0]
    q_rep = jnp.concatenate([q] * N_HEADS, axis=0)
    r_head = lax.broadcasted_iota(I32, (rows, ATTN_WIDTH), 0) // dec_seq
    c_head = lax.broadcasted_iota(I32, (rows, ATTN_WIDTH), 1) // HEAD_DIM
    q_bd = jnp.where(r_head == c_head, q_rep, 0.0).astype(BF16)
    later = _later_matrix(PAGE_SIZE)
    bias = bias_ref[...]

    def blocks(kvs, carry, mask):
        zs = [lax.dot_general(q_bd, kf.astype(BF16), (((1,), (1,)), ((), ())),
                              preferred_element_type=F32) + bias for kf, _ in kvs]
        ls = [_log_survival(z) for z in zs]
        lms = ls if mask is None else [jnp.where(mask, l, 0.0) for l in ls]
        rs = [_split_dot(lm, later) for lm in lms]
        o_acc, c = carry
        for z, l, lm, r, (_, vf) in zip(zs, ls, lms, rs, kvs):
            a = jnp.exp(z + l + r)
            if mask is not None:
                a = jnp.where(mask, a, 0.0)
            o_full = jnp.dot(a.astype(BF16), vf.astype(BF16), preferred_element_type=F32)
            o_diag = jnp.concatenate(
                [o_full[hh * dec_seq:(hh + 1) * dec_seq, hh * HEAD_DIM:(hh + 1) * HEAD_DIM]
                 for hh in range(N_HEADS)], axis=0)
            o_acc = o_acc + jnp.exp(c) * o_diag
            c = c + r[:, 0:1] + lm[:, 0:1]
        return o_acc, c

    pad = jnp.zeros((PAGE_SIZE - dec_seq, ATTN_WIDTH), F32)
    k_new = jnp.concatenate([kn_ref[0], pad], axis=0)
    v_new = jnp.concatenate([vn_ref[0], pad], axis=0)
    qi = lax.broadcasted_iota(I32, (rows, PAGE_SIZE), 0) % dec_seq
    kj = lax.broadcasted_iota(I32, (rows, PAGE_SIZE), 1)
    carry = (jnp.zeros((rows, HEAD_DIM), F32), jnp.zeros((rows, 1), F32))
    carry = blocks([(k_new, v_new)], carry, kj < qi)

    def group_step(t, carry):
        n = b * n_groups + t
        slot = n % 2
        for cp in group_copies(0, 0, slot):
            cp.wait()

        @pl.when(n + 1 < n_seq * n_groups)
        def _():
            for cp in group_copies((n + 1) // n_groups, (n + 1) % n_groups, 1 - slot):
                cp.start()

        return blocks([(kbuf[slot, g], vbuf[slot, g]) for g in range(PAGE_GROUP)], carry, None)

    o_acc, _ = lax.fori_loop(0, n_groups, group_step, carry)
    ms = jnp.mean(o_acc * o_acc, axis=-1, keepdims=True)
    o_ref[0] = o_acc * lax.rsqrt(ms + EPS) * go_ref[...]


def _attn_sample(page_ids, q_s, k_s, v_s, bias_rows, go_rows, cache_k, cache_v):
    nb, dec_seq, _ = q_s.shape
    n_pages = page_ids.shape[1]
    rows = N_HEADS * dec_seq
    seq_block = pl.BlockSpec((1, dec_seq, ATTN_WIDTH), lambda b, pt: (b, 0, 0))
    grid_spec = pltpu.PrefetchScalarGridSpec(
        num_scalar_prefetch=1,
        grid=(nb,),
        in_specs=[
            seq_block, seq_block, seq_block,
            pl.BlockSpec((rows, PAGE_SIZE), lambda b, pt: (0, 0)),
            pl.BlockSpec((rows, HEAD_DIM), lambda b, pt: (0, 0)),
            pl.BlockSpec(memory_space=pl.ANY),
            pl.BlockSpec(memory_space=pl.ANY),
        ],
        out_specs=pl.BlockSpec((1, rows, HEAD_DIM), lambda b, pt: (b, 0, 0)),
        scratch_shapes=[
            pltpu.VMEM((2, PAGE_GROUP, PAGE_SIZE, ATTN_WIDTH), F32),
            pltpu.VMEM((2, PAGE_GROUP, PAGE_SIZE, ATTN_WIDTH), F32),
            pltpu.SemaphoreType.DMA((2, 2, PAGE_GROUP)),
        ],
    )
    return pl.pallas_call(
        functools.partial(_attn_sample_kernel, n_pages=n_pages, dec_seq=dec_seq, n_seq=nb),
        grid_spec=grid_spec,
        out_shape=jax.ShapeDtypeStruct((nb, rows, HEAD_DIM), F32),
        compiler_params=_cparams(("arbitrary",)),
        name="attn_sample",
    )(page_ids.reshape(-1), q_s, k_s, v_s, bias_rows, go_rows, cache_k, cache_v)


def _conv_kernel(u_ref, hist_ref, w_ref, b_ref, g_ref, lb_ref, beta_ref, y_ref, win_ref, acc_ref, *, tt):
    i = pl.program_id(1)

    @pl.when(i == 0)
    def _():
        win_ref[0:HALO, :] = hist_ref[0]

    @pl.when(i > 0)
    def _():
        win_ref[0:HALO, :] = win_ref[tt:tt + HALO, :]

    win_ref[HALO:HALO + tt, :] = u_ref[0]
    off = HALO - HIST
    for cb in range(CONV_CH // LANES):
        cs = slice(cb * LANES, (cb + 1) * LANES)
        acc = jnp.zeros((tt, LANES), F32)
        for k in range(CONV_KERNEL):
            acc = acc + w_ref[k:k + 1, cs] * win_ref[off + k:off + k + tt, cs]
        acc_ref[:, cs] = acc
    y = acc_ref[...] + b_ref[...]
    mu = jnp.mean(y, axis=-1, keepdims=True)
    yc = y - mu
    var = jnp.mean(yc * yc, axis=-1, keepdims=True)
    y = yc * lax.rsqrt(var + EPS) * g_ref[...] + lb_ref[...]
    y_ref[0] = (y * jax.nn.sigmoid(y) * beta_ref[...]).astype(BF16)


def _conv(u, hist, w_dw, b_dw, ln_g, ln_b, beta, tt):
    nb, s, c = u.shape
    row = pl.BlockSpec((1, c), lambda b, i: (0, 0))
    return pl.pallas_call(
        functools.partial(_conv_kernel, tt=tt),
        grid=(nb, s // tt),
        in_specs=[
            pl.BlockSpec((1, tt, c), lambda b, i: (b, i, 0)),
            pl.BlockSpec((1, HALO, c), lambda b, i: (b, 0, 0)),
            pl.BlockSpec((CONV_KERNEL, c), lambda b, i: (0, 0)),
            row, row, row, row,
        ],
        out_specs=pl.BlockSpec((1, tt, c), lambda b, i: (b, i, 0)),
        out_shape=jax.ShapeDtypeStruct((nb, s, c), BF16),
        scratch_shapes=[pltpu.VMEM((HALO + tt, c), F32), pltpu.VMEM((tt, c), F32)],
        compiler_params=_cparams(("arbitrary", "arbitrary")),
        name="conv",
    )(u, hist, w_dw, b_dw, ln_g, ln_b, beta)


R_E1, R_E2, R_W1, R_W2, R_RANK1, R_RANK2 = range(6)
GRP_LANE0 = N_EXPERTS


def _lane_pick(x, lane, idx):
    return jnp.sum(jnp.where(lane == idx, x, 0.0), axis=-1, keepdims=True)


def _post_kernel(h_ref, o_ref, yc_ref, wt_ref, wb_ref, g_ref, wr_ref, br_ref,
                 h1_ref, xn_ref, route_ref, tot_ref, run_ref):
    i = pl.program_id(0)

    @pl.when(i == 0)
    def _():
        run_ref[...] = jnp.zeros_like(run_ref)

    h1 = (h_ref[...]
          + jnp.dot(o_ref[...], wt_ref[...], preferred_element_type=F32)
          + jnp.dot(yc_ref[...], wb_ref[...], preferred_element_type=F32))
    h1_ref[...] = h1
    ms = jnp.mean(h1 * h1, axis=-1, keepdims=True)
    xn = h1 * lax.rsqrt(ms + EPS) * g_ref[...]
    xn_ref[...] = xn

    x_hi = xn.astype(BF16)
    x_lo = (xn - x_hi.astype(F32)).astype(BF16)
    logits = (jnp.dot(x_hi, wr_ref[0], preferred_element_type=F32)
              + jnp.dot(x_lo, wr_ref[0], preferred_element_type=F32)
              + jnp.dot(x_hi, wr_ref[1], preferred_element_type=F32)) + br_ref[...]
    tm = logits.shape[0]
    lane = lax.broadcasted_iota(I32, (tm, LANES), 1)
    neg = -jnp.inf
    is_grp = (lane >= GRP_LANE0) & (lane < GRP_LANE0 + N_GROUPS)
    glog = jnp.where(is_grp, logits, neg)
    gmax = jnp.max(glog, axis=-1, keepdims=True)
    gidx = jnp.min(jnp.where(glog == gmax, lane - GRP_LANE0, LANES), axis=-1, keepdims=True)
    g_w = 1.0 / jnp.sum(jnp.exp(glog - gmax), axis=-1, keepdims=True)

    in_grp = (lane < N_EXPERTS) & (lane // EXPERTS_PER_GROUP == gidx)
    elog = jnp.where(in_grp, logits, neg)
    v1 = jnp.max(elog, axis=-1, keepdims=True)
    i1 = jnp.min(jnp.where(elog == v1, lane, LANES), axis=-1, keepdims=True)
    elog2 = jnp.where(lane == i1, neg, elog)
    v2 = jnp.max(elog2, axis=-1, keepdims=True)
    i2 = jnp.min(jnp.where(elog2 == v2, lane, LANES), axis=-1, keepdims=True)
    ew = jnp.exp(v2 - v1)
    w1 = g_w / (1.0 + ew)
    w2 = g_w * ew / (1.0 + ew)

    onehot = ((lane == i1) | (lane == i2)).astype(F32)
    trow = lax.broadcasted_iota(I32, (tm, tm), 0)
    tcol = lax.broadcasted_iota(I32, (tm, tm), 1)
    earlier = (tcol < trow).astype(BF16)
    before = run_ref[0:1, :] + jnp.dot(earlier, onehot.astype(BF16), preferred_element_type=F32)
    rank1 = _lane_pick(before, lane, i1)
    rank2 = _lane_pick(before, lane, i2)
    run_ref[...] = run_ref[...] + jnp.sum(onehot, axis=0, keepdims=True)
    tot_ref[...] = run_ref[...]

    rec = jnp.zeros((tm, LANES), F32)
    for idx, val in ((R_E1, i1.astype(F32)), (R_E2, i2.astype(F32)), (R_W1, w1), (R_W2, w2),
                     (R_RANK1, rank1), (R_RANK2, rank2)):
        rec = jnp.where(lane == idx, val, rec)
    route_ref[...] = rec


def _post(h_all, o_all, yc_all, w_top, w_bot, g_ffn, w_router, b_router):
    t = h_all.shape[0]
    tok = lambda w: pl.BlockSpec((TOK_TM, w), lambda i: (i, 0))
    full = lambda a: pl.BlockSpec(a.shape, lambda i: (0,) * a.ndim)
    return pl.pallas_call(
        _post_kernel,
        grid=(t // TOK_TM,),
        in_specs=[tok(D_MODEL), tok(ATTN_WIDTH), tok(CONV_CH), full(w_top), full(w_bot),
                  full(g_ffn), full(w_router), full(b_router)],
        out_specs=[tok(D_MODEL), tok(D_MODEL), tok(LANES), pl.BlockSpec((8, LANES), lambda i: (0, 0))],
        out_shape=[jax.ShapeDtypeStruct((t, D_MODEL), F32), jax.ShapeDtypeStruct((t, D_MODEL), F32),
                   jax.ShapeDtypeStruct((t, LANES), F32), jax.ShapeDtypeStruct((8, LANES), F32)],
        scratch_shapes=[pltpu.VMEM((8, LANES), F32)],
        compiler_params=_cparams(("arbitrary",)),
        name="post",
    )(h_all, o_all, yc_all, w_top, w_bot, g_ffn, w_router, b_router)


M_TILE_EXPERT, M_NUM_TILES = 0, 1


def _positions_kernel(route_ref, tot_ref, pos_ref, meta_ref):
    t = route_ref.shape[0]
    lane1 = lax.broadcasted_iota(I32, (8, LANES), 1)
    total = tot_ref[...]
    ntile = jnp.where(lane1 < N_EXPERTS, jnp.ceil(total / ROW_TM), 0.0)
    srow = lax.broadcasted_iota(I32, (LANES, LANES), 0)
    scol = lax.broadcasted_iota(I32, (LANES, LANES), 1)
    start_tile = jnp.dot(ntile.astype(BF16), (srow < scol).astype(BF16), preferred_element_type=F32)
    start_row = start_tile[0:1, :] * ROW_TM

    route = route_ref[...]
    lane = lax.broadcasted_iota(I32, (t, LANES), 1)
    e1 = _lane_pick(route, lane, R_E1).astype(I32)
    e2 = _lane_pick(route, lane, R_E2).astype(I32)
    p1 = _lane_pick(jnp.broadcast_to(start_row, (t, LANES)), lane, e1) + _lane_pick(route, lane, R_RANK1)
    p2 = _lane_pick(jnp.broadcast_to(start_row, (t, LANES)), lane, e2) + _lane_pick(route, lane, R_RANK2)
    pos_ref[...] = jnp.where(lane == 0, p1, jnp.where(lane == 1, p2, 0.0)).astype(I32)

    end_tile = start_tile[0:1, :] + ntile[0:1, :]
    end_col = jnp.sum(jnp.where(srow == scol, jnp.broadcast_to(end_tile, (LANES, LANES)), 0.0),
                      axis=-1, keepdims=True)
    done = ((end_col <= scol.astype(F32)) & (srow < N_EXPERTS)).astype(F32)
    tile_expert = jnp.minimum(jnp.sum(done, axis=0, keepdims=True), N_EXPERTS - 1.0)
    num_tiles = jnp.sum(ntile[0:1, :], axis=-1, keepdims=True)
    row8 = lax.broadcasted_iota(I32, (8, LANES), 0)
    meta = jnp.where(row8 == M_TILE_EXPERT, jnp.broadcast_to(tile_expert, (8, LANES)),
                     jnp.where(row8 == M_NUM_TILES, jnp.broadcast_to(num_tiles, (8, LANES)), 0.0))
    meta_ref[...] = meta.astype(I32)


def _positions(route, totals):
    t = route.shape[0]
    return pl.pallas_call(
        _positions_kernel,
        out_shape=[jax.ShapeDtypeStruct((t, LANES), I32), jax.ShapeDtypeStruct((8, LANES), I32)],
        compiler_params=pltpu.CompilerParams(vmem_limit_bytes=VMEM_LIMIT),
        name="positions",
    )(route, totals)


def _experts_kernel(te_ref, nt_ref, p1_ref, p2_ref, xn_hbm, wg_ref, wu_ref, wd_ref,
                    y_ref, row_tok, xg, wg_bf, wu_bf, wd_bf, sem, *, n_tok, n_rows):
    j = pl.program_id(0)
    n_tiles = nt_ref[0]

    def row_copy(r, slot, tok):
        return pltpu.make_async_copy(xn_hbm.at[pl.ds(tok, 1), :], xg.at[slot, pl.ds(r, 1), :], sem.at[slot])

    def gather(tile, slot):
        def issue(r, _):
            row_copy(r, slot, row_tok[tile * ROW_TM + r]).start()
            return 0
        lax.fori_loop(0, ROW_TM, issue, 0, unroll=DMA_UNROLL)

    @pl.when(j == 0)
    def _():
        def clear(r, _):
            row_tok[r] = 0
            return 0
        lax.fori_loop(0, n_rows, clear, 0, unroll=DMA_UNROLL)

        def place(t, _):
            row_tok[p1_ref[t]] = t
            row_tok[p2_ref[t]] = t
            return 0
        lax.fori_loop(0, n_tok, place, 0, unroll=DMA_UNROLL)
        gather(0, 0)

    @pl.when(j < n_tiles)
    def _():
        slot = j % 2

        @pl.when(j + 1 < n_tiles)
        def _():
            gather(j + 1, 1 - slot)

        new_expert = (j == 0) | (te_ref[j] != te_ref[jnp.maximum(j - 1, 0)])

        @pl.when(new_expert)
        def _():
            wg_bf[...] = wg_ref[0].astype(BF16)
            wu_bf[...] = wu_ref[0].astype(BF16)
            wd_bf[...] = wd_ref[0].astype(BF16)

        def drain(r, _):
            row_copy(r, slot, 0).wait()
            return 0
        lax.fori_loop(0, ROW_TM, drain, 0, unroll=DMA_UNROLL)

        x = xg[slot].astype(BF16)
        hg = jnp.dot(x, wg_bf[...], preferred_element_type=F32)
        hu = jnp.dot(x, wu_bf[...], preferred_element_type=F32)
        hid = (hg * jax.nn.sigmoid(hg) * hu).astype(BF16)
        y_ref[...] = jnp.dot(hid, wd_bf[...], preferred_element_type=F32)

    @pl.when(j >= nt_ref[0])
    def _():
        y_ref[...] = jnp.zeros_like(y_ref)


def _experts(tile_expert, num_tiles, pos1, pos2, xn, w_gate, w_up, w_down):
    n_tok = xn.shape[0]
    max_tiles = (2 * n_tok + N_EXPERTS * (ROW_TM - 1)) // ROW_TM
    n_rows = max_tiles * ROW_TM
    w_in_spec = pl.BlockSpec((1, D_MODEL, D_EXPERT), lambda j, te, nt, p1, p2: (te[j], 0, 0))
    grid_spec = pltpu.PrefetchScalarGridSpec(
        num_scalar_prefetch=4,
        grid=(max_tiles,),
        in_specs=[
            pl.BlockSpec(memory_space=pl.ANY),
            w_in_spec, w_in_spec,
            pl.BlockSpec((1, D_EXPERT, D_MODEL), lambda j, te, nt, p1, p2: (te[j], 0, 0)),
        ],
        out_specs=pl.BlockSpec((ROW_TM, D_MODEL), lambda j, te, nt, p1, p2: (j, 0)),
        scratch_shapes=[
            pltpu.SMEM((n_rows,), I32),
            pltpu.VMEM((2, ROW_TM, D_MODEL), F32),
            pltpu.VMEM((D_MODEL, D_EXPERT), BF16),
            pltpu.VMEM((D_MODEL, D_EXPERT), BF16),
            pltpu.VMEM((D_EXPERT, D_MODEL), BF16),
            pltpu.SemaphoreType.DMA((2,)),
        ],
    )
    return pl.pallas_call(
        functools.partial(_experts_kernel, n_tok=n_tok, n_rows=n_rows),
        grid_spec=grid_spec,
        out_shape=jax.ShapeDtypeStruct((n_rows, D_MODEL), F32),
        compiler_params=_cparams(("arbitrary",)),
        name="experts",
    )(tile_expert, num_tiles, pos1, pos2, xn, w_gate, w_up, w_down)


def _final_kernel(p1_ref, p2_ref, h1_ref, route_ref, pp_ref, ps_ref, ys_hbm, g_ref, wpg_ref, wpp_ref,
                  outp_ref, outs_ref, y1, y2, sem, *, n_prompt_tiles):
    i = pl.program_id(0)
    slot = i % 2

    def row_copies(r, slot, pos1, pos2):
        return (pltpu.make_async_copy(ys_hbm.at[pl.ds(pos1, 1), :], y1.at[slot, pl.ds(r, 1), :], sem.at[0, slot]),
                pltpu.make_async_copy(ys_hbm.at[pl.ds(pos2, 1), :], y2.at[slot, pl.ds(r, 1), :], sem.at[1, slot]))

    def gather(tile, slot):
        def issue(r, _):
            tok = tile * TOK_TM + r
            for cp in row_copies(r, slot, p1_ref[tok], p2_ref[tok]):
                cp.start()
            return 0
        lax.fori_loop(0, TOK_TM, issue, 0, unroll=DMA_UNROLL)

    @pl.when(i == 0)
    def _():
        gather(0, 0)

    @pl.when(i + 1 < pl.num_programs(0))
    def _():
        gather(i + 1, 1 - slot)

    route = route_ref[...]
    lane = lax.broadcasted_iota(I32, route.shape, 1)
    w1 = _lane_pick(route, lane, R_W1)
    w2 = _lane_pick(route, lane, R_W2)
    p = jnp.where(i < n_prompt_tiles, pp_ref[...], ps_ref[...])
    emb = jnp.dot(p.astype(BF16), wpp_ref[...], preferred_element_type=F32)

    def drain(r, _):
        for cp in row_copies(r, slot, 0, 0):
            cp.wait()
        return 0
    lax.fori_loop(0, TOK_TM, drain, 0, unroll=DMA_UNROLL)

    h2 = h1_ref[...] + w1 * y1[slot] + w2 * y2[slot]
    ms = jnp.mean(h2 * h2, axis=-1, keepdims=True)
    hn = (h2 * lax.rsqrt(ms + EPS) * g_ref[...]).astype(BF16)
    gate = jax.nn.sigmoid(jnp.dot(hn, wpg_ref[...], preferred_element_type=F32))
    res = h2 + gate * emb

    @pl.when(i < n_prompt_tiles)
    def _():
        outp_ref[...] = res

    @pl.when(i >= n_prompt_tiles)
    def _():
        outs_ref[...] = res


def _final(pos1, pos2, h1, route, p_p, p_s, ys, g_ple, w_ple_gate_bf, w_ple_proj_bf):
    t = h1.shape[0]
    n_prompt = p_p.shape[0]
    npt = n_prompt // TOK_TM
    nst = (t - n_prompt) // TOK_TM
    tok = lambda w: pl.BlockSpec((TOK_TM, w), lambda i, p1, p2: (i, 0))
    full = lambda a: pl.BlockSpec(a.shape, lambda i, p1, p2: (0,) * a.ndim)
    prompt_tile = lambda w: pl.BlockSpec((TOK_TM, w), lambda i, p1, p2: (jnp.minimum(i, npt - 1), 0))
    sample_tile = lambda w: pl.BlockSpec((TOK_TM, w), lambda i, p1, p2: (jnp.clip(i - npt, 0, nst - 1), 0))
    grid_spec = pltpu.PrefetchScalarGridSpec(
        num_scalar_prefetch=2,
        grid=(t // TOK_TM,),
        in_specs=[tok(D_MODEL), tok(LANES), prompt_tile(PLE_DIM), sample_tile(PLE_DIM),
                  pl.BlockSpec(memory_space=pl.ANY),
                  full(g_ple), full(w_ple_gate_bf), full(w_ple_proj_bf)],
        out_specs=[prompt_tile(D_MODEL), sample_tile(D_MODEL)],
        scratch_shapes=[pltpu.VMEM((2, TOK_TM, D_MODEL), F32), pltpu.VMEM((2, TOK_TM, D_MODEL), F32),
                        pltpu.SemaphoreType.DMA((2, 2))],
    )
    return pl.pallas_call(
        functools.partial(_final_kernel, n_prompt_tiles=npt),
        grid_spec=grid_spec,
        out_shape=[jax.ShapeDtypeStruct((n_prompt, D_MODEL), F32),
                   jax.ShapeDtypeStruct((t - n_prompt, D_MODEL), F32)],
        compiler_params=_cparams(("arbitrary",)),
        name="final",
    )(pos1, pos2, h1, route, p_p, p_s, ys, g_ple, w_ple_gate_bf, w_ple_proj_bf)


def _layer(i, hp, hs, cache_k, cache_v, state_conv, page_table, p_prompt, p_sample,
           g_mix_norm, w_in, g_q, g_k, sb_bias, g_o, w_dw, b_dw, ln_g, ln_b, beta_conv, w_out,
           g_ffn_norm, w_grp, b_grp, w_rt, b_rt, w_gate, w_up, w_down,
           g_ple_norm, w_ple_proj, w_ple_gate):
    batch, seq, _ = hp.shape
    nb, dec_seq, _ = hs.shape
    n_prompt = batch * seq
    n_sample = nb * dec_seq
    row = lambda a: a.reshape(1, -1)

    x_all = jnp.concatenate([hp.reshape(n_prompt, D_MODEL), hs.reshape(n_sample, D_MODEL)], axis=0)
    q_bf, k_all, v_all, kb, vb, u_all = _mix_in(
        x_all, row(g_mix_norm[i]), w_in[i].astype(BF16), row(g_q[i]), row(g_k[i]))

    o_p = _attn_prompt(sb_bias[i], q_bf, kb, vb, row(g_o[i]), batch, seq)

    rows = N_HEADS * dec_seq
    split = lambda a: a[n_prompt:].astype(F32).reshape(nb, dec_seq, ATTN_WIDTH)
    bias_rows = jnp.broadcast_to(jnp.repeat(sb_bias[i], dec_seq)[:, None], (rows, PAGE_SIZE))
    go_rows = jnp.repeat(g_o[i], dec_seq, axis=0)
    depth, n_pool = cache_k.shape[:2]
    o_s = _attn_sample(page_table + i * n_pool, split(q_bf), split(k_all), split(v_all), bias_rows, go_rows,
                       cache_k.reshape(depth * n_pool, PAGE_SIZE, ATTN_WIDTH),
                       cache_v.reshape(depth * n_pool, PAGE_SIZE, ATTN_WIDTH))
    o_s = o_s.reshape(nb, N_HEADS, dec_seq, HEAD_DIM).transpose(0, 2, 1, 3).reshape(n_sample, ATTN_WIDTH)

    conv_args = (w_dw[i], row(b_dw[i]), row(ln_g[i]), row(ln_b[i]), row(beta_conv[i]))
    u_p = u_all[:n_prompt].reshape(batch, seq, CONV_CH)
    u_s = u_all[n_prompt:].reshape(nb, dec_seq, CONV_CH)
    hist_p = jnp.zeros((batch, HALO, CONV_CH), F32)
    hist_s = jnp.pad(state_conv[i], ((0, 0), (HALO - HIST, 0), (0, 0)))
    yc_p = _conv(u_p, hist_p, *conv_args, tt=CONV_TT)
    yc_s = _conv(u_s, hist_s, *conv_args, tt=dec_seq)

    o_all = jnp.concatenate([o_p, o_s.astype(BF16)], axis=0)
    yc_all = jnp.concatenate([yc_p.reshape(n_prompt, CONV_CH), yc_s.reshape(n_sample, CONV_CH)], axis=0)
    w_out_bf = w_out[i].astype(BF16)
    w_router = jnp.zeros((D_MODEL, LANES), F32)
    w_router = w_router.at[:, :N_EXPERTS].set(w_rt[i]).at[:, GRP_LANE0:GRP_LANE0 + N_GROUPS].set(w_grp[i])
    b_router = jnp.zeros((1, LANES), F32)
    b_router = b_router.at[0, :N_EXPERTS].set(b_rt[i]).at[0, GRP_LANE0:GRP_LANE0 + N_GROUPS].set(b_grp[i])
    w_router_hi = w_router.astype(BF16)
    w_router_lo = (w_router - w_router_hi.astype(F32)).astype(BF16)
    h1, xn, route, totals = _post(x_all, o_all, yc_all, w_out_bf[:ATTN_WIDTH], w_out_bf[ATTN_WIDTH:],
                                  row(g_ffn_norm[i]), jnp.stack([w_router_hi, w_router_lo]), b_router)

    pos, meta = _positions(route, totals)
    pos1, pos2 = pos[:, 0], pos[:, 1]
    ys = _experts(meta[M_TILE_EXPERT], meta[M_NUM_TILES, :1], pos1, pos2, xn, w_gate[i], w_up[i], w_down[i])

    y_p, y_s = _final(pos1, pos2, h1, route, p_prompt[i].reshape(n_prompt, PLE_DIM),
                      p_sample[i].reshape(n_sample, PLE_DIM), ys, row(g_ple_norm[i]),
                      w_ple_gate[i].astype(BF16), w_ple_proj[i].astype(BF16))

    heads = lambda a, b_, s_: a.reshape(b_, s_, N_HEADS, HEAD_DIM)
    conv_p = u_p[:, seq - HIST:]
    conv_s = jnp.concatenate([state_conv[i], u_s], axis=1)[:, -HIST:]
    return (y_p.reshape(batch, seq, D_MODEL), y_s.reshape(nb, dec_seq, D_MODEL),
            heads(k_all[:n_prompt], batch, seq), heads(v_all[:n_prompt], batch, seq), conv_p,
            heads(k_all[n_prompt:], nb, dec_seq), heads(v_all[n_prompt:], nb, dec_seq), conv_s)


def kernel(x_prompt, x_sample, cache_k, cache_v, state_conv, page_table, p_prompt, p_sample, g_mix_norm, w_in, g_q, g_k, sb_bias, g_o, w_dw, b_dw, ln_g, ln_b, beta_conv, w_out, g_ffn_norm, w_grp, b_grp, w_rt, b_rt, w_gate, w_up, w_down, g_ple_norm, w_ple_proj, w_ple_gate):
    depth = w_in.shape[0]
    hp, hs = x_prompt, x_sample
    outs = []
    for i in range(depth):
        res = _layer(i, hp, hs, cache_k, cache_v, state_conv, page_table, p_prompt, p_sample,
                     g_mix_norm, w_in, g_q, g_k, sb_bias, g_o, w_dw, b_dw, ln_g, ln_b, beta_conv, w_out,
                     g_ffn_norm, w_grp, b_grp, w_rt, b_rt, w_gate, w_up, w_down,
                     g_ple_norm, w_ple_proj, w_ple_gate)
        hp, hs = res[0], res[1]
        outs.append(res[2:])
    stacked = tuple(jnp.stack([o[n] for o in outs], axis=0) for n in range(6))
    return (hp, hs) + stacked
```

```python
import functools

import jax
import jax.numpy as jnp
from jax import lax
from jax.experimental import pallas as pl
from jax.experimental.pallas import tpu as pltpu

F32 = jnp.float32
BF16 = jnp.bfloat16
I32 = jnp.int32

D_MODEL = 2048
HEAD_DIM = 128
N_HEADS = 8
ATTN_WIDTH = N_HEADS * HEAD_DIM
CONV_CH = D_MODEL - ATTN_WIDTH
CONV_KERNEL = 31
HIST = CONV_KERNEL - 1
N_GROUPS = 4
EXPERTS_PER_GROUP = 8
N_EXPERTS = N_GROUPS * EXPERTS_PER_GROUP
D_EXPERT = D_MODEL // 4
PLE_DIM = 256
PAGE_SIZE = 128
EPS = 1e-6

LANES = 128
VMEM_LIMIT = 56 * 1024 * 1024

MIX_TM = 768
MIX_TN = 512
ATT_T = 256
ATT_HEADS = 8
PAGE_GROUP = 4
PAGE_SLOTS = 3
CONV_TT = 256
HALO = 32
TOK_TM = 256
ROW_TM = 256
DMA_UNROLL = 8


def _cparams(sem):
    return pltpu.CompilerParams(dimension_semantics=sem, vmem_limit_bytes=VMEM_LIMIT)


def _log_survival(z):
    nz = -z
    return jnp.minimum(nz, 0.0) - jnp.log(1.0 + jnp.exp(jnp.minimum(z, nz)))


def _later_matrix(n):
    row = lax.broadcasted_iota(I32, (n, n), 0)
    col = lax.broadcasted_iota(I32, (n, n), 1)
    return (row > col).astype(BF16)


def _split_dot(l, m):
    hi = l.astype(BF16)
    lo = (l - hi.astype(F32)).astype(BF16)
    return (jnp.dot(hi, m, preferred_element_type=F32)
            + jnp.dot(lo, m, preferred_element_type=F32))


def _mix_kernel(x_ref, g_ref, wa_ref, wb_ref, gq_ref, gk_ref,
                q_ref, k_ref, v_ref, kb_ref, vb_ref, u_ref, xn_ref):
    j = pl.program_id(1)

    @pl.when(j == 0)
    def _():
        x = x_ref[...]
        ms = jnp.mean(x * x, axis=-1, keepdims=True)
        xn_ref[...] = (x * lax.rsqrt(ms + EPS) * g_ref[...]).astype(BF16)

    a = xn_ref[...]
    z = jnp.dot(a, wa_ref[...], preferred_element_type=F32)

    def head_norm(g):
        outs = []
        for hh in range(MIX_TN // HEAD_DIM):
            zh = z[:, hh * HEAD_DIM:(hh + 1) * HEAD_DIM]
            ms = jnp.mean(zh * zh, axis=-1, keepdims=True)
            outs.append(zh * lax.rsqrt(ms + EPS) * g)
        return jnp.concatenate(outs, axis=-1)

    @pl.when(j < 2)
    def _():
        q_ref[...] = (head_norm(gq_ref[...]) * (HEAD_DIM ** -0.5)).astype(BF16)

    @pl.when((j >= 2) & (j < 4))
    def _():
        kn = head_norm(gk_ref[...])
        k_ref[...] = kn
        kb_ref[...] = kn.astype(BF16)

    @pl.when((j >= 4) & (j < 6))
    def _():
        v_ref[...] = z
        vb_ref[...] = z.astype(BF16)

    @pl.when(j >= 6)
    def _():
        zb = jnp.dot(a, wb_ref[...], preferred_element_type=F32)
        u_ref[...] = z * jax.nn.sigmoid(zb)


def _mix_in(x_all, g_norm, w_in_bf, g_q, g_k):
    t = x_all.shape[0]
    nt = t // MIX_TM
    half = ATTN_WIDTH // MIX_TN

    def part_map(p):
        return lambda i, j: (i, jnp.clip(j - p * half, 0, half - 1))

    f32_out = jax.ShapeDtypeStruct((t, ATTN_WIDTH), F32)
    bf_out = jax.ShapeDtypeStruct((t, ATTN_WIDTH), BF16)
    out_block = lambda p: pl.BlockSpec((MIX_TM, MIX_TN), part_map(p))
    return pl.pallas_call(
        _mix_kernel,
        grid=(nt, 4 * half),
        in_specs=[
            pl.BlockSpec((MIX_TM, D_MODEL), lambda i, j: (i, 0)),
            pl.BlockSpec((1, D_MODEL), lambda i, j: (0, 0)),
            pl.BlockSpec((D_MODEL, MIX_TN), lambda i, j: (0, j)),
            pl.BlockSpec((D_MODEL, MIX_TN), lambda i, j: (0, jnp.maximum(j, 3 * half) + half)),
            pl.BlockSpec((1, HEAD_DIM), lambda i, j: (0, 0)),
            pl.BlockSpec((1, HEAD_DIM), lambda i, j: (0, 0)),
        ],
        out_specs=[out_block(0), out_block(1), out_block(2), out_block(1), out_block(2), out_block(3)],
        out_shape=[bf_out, f32_out, f32_out, bf_out, bf_out, f32_out],
        scratch_shapes=[pltpu.VMEM((MIX_TM, D_MODEL), BF16)],
        compiler_params=_cparams(("arbitrary", "arbitrary")),
        name="mix_in",
    )(x_all, g_norm, w_in_bf, w_in_bf, g_q, g_k)


def _attn_prompt_kernel(bias_ref, q_ref, k_ref, v_ref, go_ref, o_ref):
    hg = pl.program_id(1)
    i = pl.program_id(2)
    later = _later_matrix(ATT_T)
    heads = range(ATT_HEADS)
    hs = [slice(n * HEAD_DIM, (n + 1) * HEAD_DIM) for n in heads]
    bias = [bias_ref[hg * ATT_HEADS + n] for n in heads]
    qs = [q_ref[:, hs[n]] for n in heads]
    causal = (lax.broadcasted_iota(I32, (ATT_T, ATT_T), 1) < lax.broadcasted_iota(I32, (ATT_T, ATT_T), 0))

    def block(kb, carry, masked):
        ks = pl.multiple_of(kb * ATT_T, ATT_T)
        zs = [lax.dot_general(qs[n], k_ref[pl.ds(ks, ATT_T), hs[n]], (((1,), (1,)), ((), ())),
                              preferred_element_type=F32) + bias[n] for n in heads]
        ls = [_log_survival(z) for z in zs]
        lms = [jnp.where(causal, l, 0.0) for l in ls] if masked else ls
        rs = [_split_dot(lm, later) for lm in lms]
        avs = []
        for n in heads:
            a = jnp.exp(zs[n] + ls[n] + rs[n])
            if masked:
                a = jnp.where(causal, a, 0.0)
            avs.append(jnp.dot(a.astype(BF16), v_ref[pl.ds(ks, ATT_T), hs[n]], preferred_element_type=F32))
        return tuple((carry[n][0] + jnp.exp(carry[n][1]) * avs[n],
                      carry[n][1] + rs[n][:, 0:1] + lms[n][:, 0:1]) for n in heads)

    carry = tuple((jnp.zeros((ATT_T, HEAD_DIM), F32), jnp.zeros((ATT_T, 1), F32)) for _ in heads)
    carry = block(i, carry, True)
    carry = lax.fori_loop(0, i, lambda t, cr: block(i - 1 - t, cr, False), carry)
    for n in heads:
        o_acc = carry[n][0]
        ms = jnp.mean(o_acc * o_acc, axis=-1, keepdims=True)
        o_ref[:, hs[n]] = (o_acc * lax.rsqrt(ms + EPS) * go_ref[:, hs[n]]).astype(BF16)


def _attn_prompt(sb_bias, q_bf, kb, vb, g_o_row, batch, seq):
    nq = seq // ATT_T
    width = ATT_HEADS * HEAD_DIM
    return pl.pallas_call(
        _attn_prompt_kernel,
        grid=(batch, N_HEADS // ATT_HEADS, nq),
        in_specs=[
            pl.BlockSpec(memory_space=pltpu.SMEM),
            pl.BlockSpec((ATT_T, width), lambda b, h, i: (b * nq + i, h)),
            pl.BlockSpec((seq, width), lambda b, h, i: (b, h)),
            pl.BlockSpec((seq, width), lambda b, h, i: (b, h)),
            pl.BlockSpec((1, width), lambda b, h, i: (0, h)),
        ],
        out_specs=pl.BlockSpec((ATT_T, width), lambda b, h, i: (b * nq + i, h)),
        out_shape=jax.ShapeDtypeStruct((batch * seq, ATTN_WIDTH), BF16),
        compiler_params=_cparams(("arbitrary", "arbitrary", "arbitrary")),
        name="attn_prompt",
    )(sb_bias, q_bf, kb, vb, g_o_row)


def _attn_sample_kernel(pt_ref, q_ref, kn_ref, vn_ref, bias_ref, go_ref, ck_hbm, cv_hbm,
                        o_ref, kbuf, vbuf, sem, *, n_pages, dec_seq, n_seq):
    b = pl.program_id(0)
    rows = N_HEADS * dec_seq
    n_groups = n_pages // PAGE_GROUP

    def group_copies(seq, grp, slot):
        cps = []
        for g in range(PAGE_GROUP):
            page = pt_ref[seq * n_pages + n_pages - 1 - (grp * PAGE_GROUP + g)]
            cps.append(pltpu.make_async_copy(ck_hbm.at[page], kbuf.at[slot, g], sem.at[0, slot, g]))
            cps.append(pltpu.make_async_copy(cv_hbm.at[page], vbuf.at[slot, g], sem.at[1, slot, g]))
        return cps

    def start_group(n):
        @pl.when(n < n_seq * n_groups)
        def _():
            for cp in group_copies(n // n_groups, n % n_groups, n % PAGE_SLOTS):
                cp.start()

    @pl.when(b == 0)
    def _():
        for n in range(PAGE_SLOTS - 1):
            start_group(n)

    heads = range(N_HEADS)
    hs = [slice(h * HEAD_DIM, (h + 1) * HEAD_DIM) for h in heads]
    qr = [slice(h * dec_seq, (h + 1) * dec_seq) for h in heads]
    q = q_ref[0]
    q_h = [q[:, hs[h]].astype(BF16) for h in heads]
    later = _later_matrix(PAGE_SIZE)
    bias = bias_ref[...]

    def blocks(k_of, v_of, n_blocks, carry, mask):
        blks = range(n_blocks)
        zs = [jnp.concatenate(
            [lax.dot_general(q_h[h], k_of(blk, h).astype(BF16), (((1,), (1,)), ((), ())),
                             preferred_element_type=F32) for h in heads], axis=0) + bias for blk in blks]
        ls = [_log_survival(z) for z in zs]
        lms = ls if mask is None else [jnp.where(mask, l, 0.0) for l in ls]
        rs = [_split_dot(lm, later) for lm in lms]
        o_acc, c = carry
        for blk in blks:
            a = jnp.exp(zs[blk] + ls[blk] + rs[blk])
            if mask is not None:
                a = jnp.where(mask, a, 0.0)
            o_blk = jnp.concatenate(
                [jnp.dot(a[qr[h]].astype(BF16), v_of(blk, h).astype(BF16), preferred_element_type=F32)
                 for h in heads], axis=0)
            o_acc = o_acc + jnp.exp(c) * o_blk
            c = c + rs[blk][:, 0:1] + lms[blk][:, 0:1]
        return o_acc, c

    pad = jnp.zeros((PAGE_SIZE - dec_seq, HEAD_DIM), F32)
    k_new, v_new = kn_ref[0], vn_ref[0]
    qi = lax.broadcasted_iota(I32, (rows, PAGE_SIZE), 0) % dec_seq
    kj = lax.broadcasted_iota(I32, (rows, PAGE_SIZE), 1)
    carry = (jnp.zeros((rows, HEAD_DIM), F32), jnp.zeros((rows, 1), F32))
    carry = blocks(lambda blk, h: jnp.concatenate([k_new[:, hs[h]], pad], axis=0),
                   lambda blk, h: jnp.concatenate([v_new[:, hs[h]], pad], axis=0), 1, carry, kj < qi)

    def group_step(t, carry):
        n = b * n_groups + t
        slot = n % PAGE_SLOTS
        for cp in group_copies(0, 0, slot):
            cp.wait()
        start_group(n + PAGE_SLOTS - 1)

        def head_rows(buf):
            return lambda g, h: buf[slot, g, pl.ds(h, PAGE_SIZE, stride=N_HEADS), :]
        return blocks(head_rows(kbuf), head_rows(vbuf), PAGE_GROUP, carry, None)

    o_acc, _ = lax.fori_loop(0, n_groups, group_step, carry)
    ms = jnp.mean(o_acc * o_acc, axis=-1, keepdims=True)
    o_ref[0] = o_acc * lax.rsqrt(ms + EPS) * go_ref[...]


def _attn_sample(page_ids, q_s, k_s, v_s, bias_rows, go_rows, cache_k, cache_v):
    nb, dec_seq, _ = q_s.shape
    n_pages = page_ids.shape[1]
    rows = N_HEADS * dec_seq
    seq_block = pl.BlockSpec((1, dec_seq, ATTN_WIDTH), lambda b, pt: (b, 0, 0))
    grid_spec = pltpu.PrefetchScalarGridSpec(
        num_scalar_prefetch=1,
        grid=(nb,),
        in_specs=[
            seq_block, seq_block, seq_block,
            pl.BlockSpec((rows, PAGE_SIZE), lambda b, pt: (0, 0)),
            pl.BlockSpec((rows, HEAD_DIM), lambda b, pt: (0, 0)),
            pl.BlockSpec(memory_space=pl.ANY),
            pl.BlockSpec(memory_space=pl.ANY),
        ],
        out_specs=pl.BlockSpec((1, rows, HEAD_DIM), lambda b, pt: (b, 0, 0)),
        scratch_shapes=[
            pltpu.VMEM((PAGE_SLOTS, PAGE_GROUP, PAGE_SIZE * N_HEADS, HEAD_DIM), F32),
            pltpu.VMEM((PAGE_SLOTS, PAGE_GROUP, PAGE_SIZE * N_HEADS, HEAD_DIM), F32),
            pltpu.SemaphoreType.DMA((2, PAGE_SLOTS, PAGE_GROUP)),
        ],
    )
    return pl.pallas_call(
        functools.partial(_attn_sample_kernel, n_pages=n_pages, dec_seq=dec_seq, n_seq=nb),
        grid_spec=grid_spec,
        out_shape=jax.ShapeDtypeStruct((nb, rows, HEAD_DIM), F32),
        compiler_params=_cparams(("arbitrary",)),
        name="attn_sample",
    )(page_ids.reshape(-1), q_s, k_s, v_s, bias_rows, go_rows, cache_k, cache_v)


def _conv_kernel(u_ref, hist_ref, w_ref, b_ref, g_ref, lb_ref, beta_ref, y_ref, win_ref, acc_ref, *, tt):
    i = pl.program_id(1)

    @pl.when(i == 0)
    def _():
        win_ref[0:HALO, :] = hist_ref[0]

    @pl.when(i > 0)
    def _():
        win_ref[0:HALO, :] = win_ref[tt:tt + HALO, :]

    win_ref[HALO:HALO + tt, :] = u_ref[0]
    off = HALO - HIST
    for cb in range(CONV_CH // LANES):
        cs = slice(cb * LANES, (cb + 1) * LANES)
        acc = jnp.zeros((tt, LANES), F32)
        for k in range(CONV_KERNEL):
            acc = acc + w_ref[k:k + 1, cs] * win_ref[off + k:off + k + tt, cs]
        acc_ref[:, cs] = acc
    y = acc_ref[...] + b_ref[...]
    mu = jnp.mean(y, axis=-1, keepdims=True)
    yc = y - mu
    var = jnp.mean(yc * yc, axis=-1, keepdims=True)
    y = yc * lax.rsqrt(var + EPS) * g_ref[...] + lb_ref[...]
    y_ref[0] = (y * jax.nn.sigmoid(y) * beta_ref[...]).astype(BF16)


def _conv(u, hist, w_dw, b_dw, ln_g, ln_b, beta, tt):
    nb, s, c = u.shape
    row = pl.BlockSpec((1, c), lambda b, i: (0, 0))
    return pl.pallas_call(
        functools.partial(_conv_kernel, tt=tt),
        grid=(nb, s // tt),
        in_specs=[
            pl.BlockSpec((1, tt, c), lambda b, i: (b, i, 0)),
            pl.BlockSpec((1, HALO, c), lambda b, i: (b, 0, 0)),
            pl.BlockSpec((CONV_KERNEL, c), lambda b, i: (0, 0)),
            row, row, row, row,
        ],
        out_specs=pl.BlockSpec((1, tt, c), lambda b, i: (b, i, 0)),
        out_shape=jax.ShapeDtypeStruct((nb, s, c), BF16),
        scratch_shapes=[pltpu.VMEM((HALO + tt, c), F32), pltpu.VMEM((tt, c), F32)],
        compiler_params=_cparams(("arbitrary", "arbitrary")),
        name="conv",
    )(u, hist, w_dw, b_dw, ln_g, ln_b, beta)


R_E1, R_E2, R_W1, R_W2, R_RANK1, R_RANK2 = range(6)
GRP_LANE0 = N_EXPERTS


def _lane_pick(x, lane, idx):
    return jnp.sum(jnp.where(lane == idx, x, 0.0), axis=-1, keepdims=True)


def _post_kernel(h_ref, o_ref, yc_ref, wt_ref, wb_ref, g_ref, wr_ref, br_ref,
                 h1_ref, xn_ref, route_ref, tot_ref, run_ref):
    i = pl.program_id(0)

    @pl.when(i == 0)
    def _():
        run_ref[...] = jnp.zeros_like(run_ref)

    h1 = (h_ref[...]
          + jnp.dot(o_ref[...], wt_ref[...], preferred_element_type=F32)
          + jnp.dot(yc_ref[...], wb_ref[...], preferred_element_type=F32))
    h1_ref[...] = h1
    ms = jnp.mean(h1 * h1, axis=-1, keepdims=True)
    xn = h1 * lax.rsqrt(ms + EPS) * g_ref[...]
    xn_ref[...] = xn

    x_hi = xn.astype(BF16)
    x_lo = (xn - x_hi.astype(F32)).astype(BF16)
    logits = (jnp.dot(x_hi, wr_ref[0], preferred_element_type=F32)
              + jnp.dot(x_lo, wr_ref[0], preferred_element_type=F32)
              + jnp.dot(x_hi, wr_ref[1], preferred_element_type=F32)) + br_ref[...]
    tm = logits.shape[0]
    lane = lax.broadcasted_iota(I32, (tm, LANES), 1)
    neg = -jnp.inf
    is_grp = (lane >= GRP_LANE0) & (lane < GRP_LANE0 + N_GROUPS)
    glog = jnp.where(is_grp, logits, neg)
    gmax = jnp.max(glog, axis=-1, keepdims=True)
    gidx = jnp.min(jnp.where(glog == gmax, lane - GRP_LANE0, LANES), axis=-1, keepdims=True)
    g_w = 1.0 / jnp.sum(jnp.exp(glog - gmax), axis=-1, keepdims=True)

    in_grp = (lane < N_EXPERTS) & (lane // EXPERTS_PER_GROUP == gidx)
    elog = jnp.where(in_grp, logits, neg)
    v1 = jnp.max(elog, axis=-1, keepdims=True)
    i1 = jnp.min(jnp.where(elog == v1, lane, LANES), axis=-1, keepdims=True)
    elog2 = jnp.where(lane == i1, neg, elog)
    v2 = jnp.max(elog2, axis=-1, keepdims=True)
    i2 = jnp.min(jnp.where(elog2 == v2, lane, LANES), axis=-1, keepdims=True)
    ew = jnp.exp(v2 - v1)
    w1 = g_w / (1.0 + ew)
    w2 = g_w * ew / (1.0 + ew)

    onehot = ((lane == i1) | (lane == i2)).astype(F32)
    trow = lax.broadcasted_iota(I32, (tm, tm), 0)
    tcol = lax.broadcasted_iota(I32, (tm, tm), 1)
    earlier = (tcol < trow).astype(BF16)
    before = run_ref[0:1, :] + jnp.dot(earlier, onehot.astype(BF16), preferred_element_type=F32)
    rank1 = _lane_pick(before, lane, i1)
    rank2 = _lane_pick(before, lane, i2)
    run_ref[...] = run_ref[...] + jnp.sum(onehot, axis=0, keepdims=True)
    tot_ref[...] = run_ref[...]

    rec = jnp.zeros((tm, LANES), F32)
    for idx, val in ((R_E1, i1.astype(F32)), (R_E2, i2.astype(F32)), (R_W1, w1), (R_W2, w2),
                     (R_RANK1, rank1), (R_RANK2, rank2)):
        rec = jnp.where(lane == idx, val, rec)
    route_ref[...] = rec


def _post(h_all, o_all, yc_all, w_top, w_bot, g_ffn, w_router, b_router):
    t = h_all.shape[0]
    tok = lambda w: pl.BlockSpec((TOK_TM, w), lambda i: (i, 0))
    full = lambda a: pl.BlockSpec(a.shape, lambda i: (0,) * a.ndim)
    return pl.pallas_call(
        _post_kernel,
        grid=(t // TOK_TM,),
        in_specs=[tok(D_MODEL), tok(ATTN_WIDTH), tok(CONV_CH), full(w_top), full(w_bot),
                  full(g_ffn), full(w_router), full(b_router)],
        out_specs=[tok(D_MODEL), tok(D_MODEL), tok(LANES), pl.BlockSpec((8, LANES), lambda i: (0, 0))],
        out_shape=[jax.ShapeDtypeStruct((t, D_MODEL), F32), jax.ShapeDtypeStruct((t, D_MODEL), F32),
                   jax.ShapeDtypeStruct((t, LANES), F32), jax.ShapeDtypeStruct((8, LANES), F32)],
        scratch_shapes=[pltpu.VMEM((8, LANES), F32)],
        compiler_params=_cparams(("arbitrary",)),
        name="post",
    )(h_all, o_all, yc_all, w_top, w_bot, g_ffn, w_router, b_router)


M_TILE_EXPERT, M_NUM_TILES = 0, 1


def _positions_kernel(route_ref, tot_ref, pos_ref, meta_ref):
    t = route_ref.shape[0]
    lane1 = lax.broadcasted_iota(I32, (8, LANES), 1)
    total = tot_ref[...]
    ntile = jnp.where(lane1 < N_EXPERTS, jnp.ceil(total / ROW_TM), 0.0)
    srow = lax.broadcasted_iota(I32, (LANES, LANES), 0)
    scol = lax.broadcasted_iota(I32, (LANES, LANES), 1)
    start_tile = jnp.dot(ntile.astype(BF16), (srow < scol).astype(BF16), preferred_element_type=F32)
    start_row = start_tile[0:1, :] * ROW_TM

    route = route_ref[...]
    lane = lax.broadcasted_iota(I32, (t, LANES), 1)
    e1 = _lane_pick(route, lane, R_E1).astype(I32)
    e2 = _lane_pick(route, lane, R_E2).astype(I32)
    p1 = _lane_pick(jnp.broadcast_to(start_row, (t, LANES)), lane, e1) + _lane_pick(route, lane, R_RANK1)
    p2 = _lane_pick(jnp.broadcast_to(start_row, (t, LANES)), lane, e2) + _lane_pick(route, lane, R_RANK2)
    pos_ref[...] = jnp.where(lane == 0, p1, jnp.where(lane == 1, p2, 0.0)).astype(I32)

    end_tile = start_tile[0:1, :] + ntile[0:1, :]
    end_col = jnp.sum(jnp.where(srow == scol, jnp.broadcast_to(end_tile, (LANES, LANES)), 0.0),
                      axis=-1, keepdims=True)
    done = ((end_col <= scol.astype(F32)) & (srow < N_EXPERTS)).astype(F32)
    tile_expert = jnp.minimum(jnp.sum(done, axis=0, keepdims=True), N_EXPERTS - 1.0)
    num_tiles = jnp.sum(ntile[0:1, :], axis=-1, keepdims=True)
    row8 = lax.broadcasted_iota(I32, (8, LANES), 0)
    meta = jnp.where(row8 == M_TILE_EXPERT, jnp.broadcast_to(tile_expert, (8, LANES)),
                     jnp.where(row8 == M_NUM_TILES, jnp.broadcast_to(num_tiles, (8, LANES)), 0.0))
    meta_ref[...] = meta.astype(I32)


def _positions(route, totals):
    t = route.shape[0]
    return pl.pallas_call(
        _positions_kernel,
        out_shape=[jax.ShapeDtypeStruct((t, LANES), I32), jax.ShapeDtypeStruct((8, LANES), I32)],
        compiler_params=pltpu.CompilerParams(vmem_limit_bytes=VMEM_LIMIT),
        name="positions",
    )(route, totals)


def _experts_kernel(te_ref, nt_ref, p1_ref, p2_ref, xn_hbm, wg_ref, wu_ref, wd_ref,
                    y_ref, row_tok, xg, wg_bf, wu_bf, wd_bf, sem, *, n_tok, n_rows):
    j = pl.program_id(0)
    n_tiles = nt_ref[0]

    def row_copy(r, slot, tok):
        return pltpu.make_async_copy(xn_hbm.at[pl.ds(tok, 1), :], xg.at[slot, pl.ds(r, 1), :], sem.at[slot])

    def gather(tile, slot):
        def issue(r, _):
            row_copy(r, slot, row_tok[tile * ROW_TM + r]).start()
            return 0
        lax.fori_loop(0, ROW_TM, issue, 0, unroll=DMA_UNROLL)

    @pl.when(j == 0)
    def _():
        def clear(r, _):
            row_tok[r] = 0
            return 0
        lax.fori_loop(0, n_rows, clear, 0, unroll=DMA_UNROLL)

        def place(t, _):
            row_tok[p1_ref[t]] = t
            row_tok[p2_ref[t]] = t
            return 0
        lax.fori_loop(0, n_tok, place, 0, unroll=DMA_UNROLL)
        gather(0, 0)

    @pl.when(j < n_tiles)
    def _():
        slot = j % 2

        @pl.when(j + 1 < n_tiles)
        def _():
            gather(j + 1, 1 - slot)

        new_expert = (j == 0) | (te_ref[j] != te_ref[jnp.maximum(j - 1, 0)])

        @pl.when(new_expert)
        def _():
            wg_bf[...] = wg_ref[0].astype(BF16)
            wu_bf[...] = wu_ref[0].astype(BF16)
            wd_bf[...] = wd_ref[0].astype(BF16)

        def drain(r, _):
            row_copy(r, slot, 0).wait()
            return 0
        lax.fori_loop(0, ROW_TM, drain, 0, unroll=DMA_UNROLL)

        x = xg[slot].astype(BF16)
        hg = jnp.dot(x, wg_bf[...], preferred_element_type=F32)
        hu = jnp.dot(x, wu_bf[...], preferred_element_type=F32)
        hid = (hg * jax.nn.sigmoid(hg) * hu).astype(BF16)
        y_ref[...] = jnp.dot(hid, wd_bf[...], preferred_element_type=F32)

    @pl.when(j >= nt_ref[0])
    def _():
        y_ref[...] = jnp.zeros_like(y_ref)


def _experts(tile_expert, num_tiles, pos1, pos2, xn, w_gate, w_up, w_down):
    n_tok = xn.shape[0]
    max_tiles = (2 * n_tok + N_EXPERTS * (ROW_TM - 1)) // ROW_TM
    n_rows = max_tiles * ROW_TM
    w_in_spec = pl.BlockSpec((1, D_MODEL, D_EXPERT), lambda j, te, nt, p1, p2: (te[j], 0, 0))
    grid_spec = pltpu.PrefetchScalarGridSpec(
        num_scalar_prefetch=4,
        grid=(max_tiles,),
        in_specs=[
            pl.BlockSpec(memory_space=pl.ANY),
            w_in_spec, w_in_spec,
            pl.BlockSpec((1, D_EXPERT, D_MODEL), lambda j, te, nt, p1, p2: (te[j], 0, 0)),
        ],
        out_specs=pl.BlockSpec((ROW_TM, D_MODEL), lambda j, te, nt, p1, p2: (j, 0)),
        scratch_shapes=[
            pltpu.SMEM((n_rows,), I32),
            pltpu.VMEM((2, ROW_TM, D_MODEL), F32),
            pltpu.VMEM((D_MODEL, D_EXPERT), BF16),
            pltpu.VMEM((D_MODEL, D_EXPERT), BF16),
            pltpu.VMEM((D_EXPERT, D_MODEL), BF16),
            pltpu.SemaphoreType.DMA((2,)),
        ],
    )
    return pl.pallas_call(
        functools.partial(_experts_kernel, n_tok=n_tok, n_rows=n_rows),
        grid_spec=grid_spec,
        out_shape=jax.ShapeDtypeStruct((n_rows, D_MODEL), F32),
        compiler_params=_cparams(("arbitrary",)),
        name="experts",
    )(tile_expert, num_tiles, pos1, pos2, xn, w_gate, w_up, w_down)


def _final_kernel(p1_ref, p2_ref, h1_ref, route_ref, pp_ref, ps_ref, ys_hbm, g_ref, wpg_ref, wpp_ref,
                  outp_ref, outs_ref, y1, y2, sem, *, n_prompt_tiles):
    i = pl.program_id(0)
    slot = i % 2

    def row_copies(r, slot, pos1, pos2):
        return (pltpu.make_async_copy(ys_hbm.at[pl.ds(pos1, 1), :], y1.at[slot, pl.ds(r, 1), :], sem.at[0, slot]),
                pltpu.make_async_copy(ys_hbm.at[pl.ds(pos2, 1), :], y2.at[slot, pl.ds(r, 1), :], sem.at[1, slot]))

    def gather(tile, slot):
        def issue(r, _):
            tok = tile * TOK_TM + r
            for cp in row_copies(r, slot, p1_ref[tok], p2_ref[tok]):
                cp.start()
            return 0
        lax.fori_loop(0, TOK_TM, issue, 0, unroll=DMA_UNROLL)

    @pl.when(i == 0)
    def _():
        gather(0, 0)

    @pl.when(i + 1 < pl.num_programs(0))
    def _():
        gather(i + 1, 1 - slot)

    route = route_ref[...]
    lane = lax.broadcasted_iota(I32, route.shape, 1)
    w1 = _lane_pick(route, lane, R_W1)
    w2 = _lane_pick(route, lane, R_W2)
    p = jnp.where(i < n_prompt_tiles, pp_ref[...], ps_ref[...])
    emb = jnp.dot(p.astype(BF16), wpp_ref[...], preferred_element_type=F32)

    def drain(r, _):
        for cp in row_copies(r, slot, 0, 0):
            cp.wait()
        return 0
    lax.fori_loop(0, TOK_TM, drain, 0, unroll=DMA_UNROLL)

    h2 = h1_ref[...] + w1 * y1[slot] + w2 * y2[slot]
    ms = jnp.mean(h2 * h2, axis=-1, keepdims=True)
    hn = (h2 * lax.rsqrt(ms + EPS) * g_ref[...]).astype(BF16)
    gate = jax.nn.sigmoid(jnp.dot(hn, wpg_ref[...], preferred_element_type=F32))
    res = h2 + gate * emb

    @pl.when(i < n_prompt_tiles)
    def _():
        outp_ref[...] = res

    @pl.when(i >= n_prompt_tiles)
    def _():
        outs_ref[...] = res


def _final(pos1, pos2, h1, route, p_p, p_s, ys, g_ple, w_ple_gate_bf, w_ple_proj_bf):
    t = h1.shape[0]
    n_prompt = p_p.shape[0]
    npt = n_prompt // TOK_TM
    nst = (t - n_prompt) // TOK_TM
    tok = lambda w: pl.BlockSpec((TOK_TM, w), lambda i, p1, p2: (i, 0))
    full = lambda a: pl.BlockSpec(a.shape, lambda i, p1, p2: (0,) * a.ndim)
    prompt_tile = lambda w: pl.BlockSpec((TOK_TM, w), lambda i, p1, p2: (jnp.minimum(i, npt - 1), 0))
    sample_tile = lambda w: pl.BlockSpec((TOK_TM, w), lambda i, p1, p2: (jnp.clip(i - npt, 0, nst - 1), 0))
    grid_spec = pltpu.PrefetchScalarGridSpec(
        num_scalar_prefetch=2,
        grid=(t // TOK_TM,),
        in_specs=[tok(D_MODEL), tok(LANES), prompt_tile(PLE_DIM), sample_tile(PLE_DIM),
                  pl.BlockSpec(memory_space=pl.ANY),
                  full(g_ple), full(w_ple_gate_bf), full(w_ple_proj_bf)],
        out_specs=[prompt_tile(D_MODEL), sample_tile(D_MODEL)],
        scratch_shapes=[pltpu.VMEM((2, TOK_TM, D_MODEL), F32), pltpu.VMEM((2, TOK_TM, D_MODEL), F32),
                        pltpu.SemaphoreType.DMA((2, 2))],
    )
    return pl.pallas_call(
        functools.partial(_final_kernel, n_prompt_tiles=npt),
        grid_spec=grid_spec,
        out_shape=[jax.ShapeDtypeStruct((n_prompt, D_MODEL), F32),
                   jax.ShapeDtypeStruct((t - n_prompt, D_MODEL), F32)],
        compiler_params=_cparams(("arbitrary",)),
        name="final",
    )(pos1, pos2, h1, route, p_p, p_s, ys, g_ple, w_ple_gate_bf, w_ple_proj_bf)


def _layer(i, hp, hs, cache_k, cache_v, state_conv, page_table, p_prompt, p_sample,
           g_mix_norm, w_in, g_q, g_k, sb_bias, g_o, w_dw, b_dw, ln_g, ln_b, beta_conv, w_out,
           g_ffn_norm, w_grp, b_grp, w_rt, b_rt, w_gate, w_up, w_down,
           g_ple_norm, w_ple_proj, w_ple_gate):
    batch, seq, _ = hp.shape
    nb, dec_seq, _ = hs.shape
    n_prompt = batch * seq
    n_sample = nb * dec_seq
    row = lambda a: a.reshape(1, -1)

    x_all = jnp.concatenate([hp.reshape(n_prompt, D_MODEL), hs.reshape(n_sample, D_MODEL)], axis=0)
    q_bf, k_all, v_all, kb, vb, u_all = _mix_in(
        x_all, row(g_mix_norm[i]), w_in[i].astype(BF16), row(g_q[i]), row(g_k[i]))

    o_p = _attn_prompt(sb_bias[i], q_bf, kb, vb, row(g_o[i]), batch, seq)

    rows = N_HEADS * dec_seq
    split = lambda a: a[n_prompt:].astype(F32).reshape(nb, dec_seq, ATTN_WIDTH)
    bias_rows = jnp.broadcast_to(jnp.repeat(sb_bias[i], dec_seq)[:, None], (rows, PAGE_SIZE))
    go_rows = jnp.repeat(g_o[i], dec_seq, axis=0)
    depth, n_pool = cache_k.shape[:2]
    o_s = _attn_sample(page_table + i * n_pool, split(q_bf), split(k_all), split(v_all), bias_rows, go_rows,
                       cache_k.reshape(depth * n_pool, PAGE_SIZE * N_HEADS, HEAD_DIM),
                       cache_v.reshape(depth * n_pool, PAGE_SIZE * N_HEADS, HEAD_DIM))
    o_s = o_s.reshape(nb, N_HEADS, dec_seq, HEAD_DIM).transpose(0, 2, 1, 3).reshape(n_sample, ATTN_WIDTH)

    conv_args = (w_dw[i], row(b_dw[i]), row(ln_g[i]), row(ln_b[i]), row(beta_conv[i]))
    u_p = u_all[:n_prompt].reshape(batch, seq, CONV_CH)
    u_s = u_all[n_prompt:].reshape(nb, dec_seq, CONV_CH)
    hist_p = jnp.zeros((batch, HALO, CONV_CH), F32)
    hist_s = jnp.pad(state_conv[i], ((0, 0), (HALO - HIST, 0), (0, 0)))
    yc_p = _conv(u_p, hist_p, *conv_args, tt=CONV_TT)
    yc_s = _conv(u_s, hist_s, *conv_args, tt=dec_seq)

    o_all = jnp.concatenate([o_p, o_s.astype(BF16)], axis=0)
    yc_all = jnp.concatenate([yc_p.reshape(n_prompt, CONV_CH), yc_s.reshape(n_sample, CONV_CH)], axis=0)
    w_out_bf = w_out[i].astype(BF16)
    w_router = jnp.zeros((D_MODEL, LANES), F32)
    w_router = w_router.at[:, :N_EXPERTS].set(w_rt[i]).at[:, GRP_LANE0:GRP_LANE0 + N_GROUPS].set(w_grp[i])
    b_router = jnp.zeros((1, LANES), F32)
    b_router = b_router.at[0, :N_EXPERTS].set(b_rt[i]).at[0, GRP_LANE0:GRP_LANE0 + N_GROUPS].set(b_grp[i])
    w_router_hi = w_router.astype(BF16)
    w_router_lo = (w_router - w_router_hi.astype(F32)).astype(BF16)
    h1, xn, route, totals = _post(x_all, o_all, yc_all, w_out_bf[:ATTN_WIDTH], w_out_bf[ATTN_WIDTH:],
                                  row(g_ffn_norm[i]), jnp.stack([w_router_hi, w_router_lo]), b_router)

    pos, meta = _positions(route, totals)
    pos1, pos2 = pos[:, 0], pos[:, 1]
    ys = _experts(meta[M_TILE_EXPERT], meta[M_NUM_TILES, :1], pos1, pos2, xn, w_gate[i], w_up[i], w_down[i])

    y_p, y_s = _final(pos1, pos2, h1, route, p_prompt[i].reshape(n_prompt, PLE_DIM),
                      p_sample[i].reshape(n_sample, PLE_DIM), ys, row(g_ple_norm[i]),
                      w_ple_gate[i].astype(BF16), w_ple_proj[i].astype(BF16))

    heads = lambda a, b_, s_: a.reshape(b_, s_, N_HEADS, HEAD_DIM)
    conv_p = u_p[:, seq - HIST:]
    conv_s = jnp.concatenate([state_conv[i], u_s], axis=1)[:, -HIST:]
    return (y_p.reshape(batch, seq, D_MODEL), y_s.reshape(nb, dec_seq, D_MODEL),
            heads(k_all[:n_prompt], batch, seq), heads(v_all[:n_prompt], batch, seq), conv_p,
            heads(k_all[n_prompt:], nb, dec_seq), heads(v_all[n_prompt:], nb, dec_seq), conv_s)


def kernel(x_prompt, x_sample, cache_k, cache_v, state_conv, page_table, p_prompt, p_sample, g_mix_norm, w_in, g_q, g_k, sb_bias, g_o, w_dw, b_dw, ln_g, ln_b, beta_conv, w_out, g_ffn_norm, w_grp, b_grp, w_rt, b_rt, w_gate, w_up, w_down, g_ple_norm, w_ple_proj, w_ple_gate):
    depth = w_in.shape[0]
    hp, hs = x_prompt, x_sample
    outs = []
    for i in range(depth):
        res = _layer(i, hp, hs, cache_k, cache_v, state_conv, page_table, p_prompt, p_sample,
                     g_mix_norm, w_in, g_q, g_k, sb_bias, g_o, w_dw, b_dw, ln_g, ln_b, beta_conv, w_out,
                     g_ffn_norm, w_grp, b_grp, w_rt, b_rt, w_gate, w_up, w_down,
                     g_ple_norm, w_ple_proj, w_ple_gate)
        hp, hs = res[0], res[1]
        outs.append(res[2:])
    stacked = tuple(jnp.stack([o[n] for o in outs], axis=0) for n in range(6))
    return (hp, hs) + stacked
```

```python
import functools

import jax
import jax.numpy as jnp
from jax import lax
from jax.experimental import pallas as pl
from jax.experimental.pallas import tpu as pltpu

F32 = jnp.float32
BF16 = jnp.bfloat16
I32 = jnp.int32

D_MODEL = 2048
HEAD_DIM = 128
N_HEADS = 8
ATTN_WIDTH = N_HEADS * HEAD_DIM
CONV_CH = D_MODEL - ATTN_WIDTH
CONV_KERNEL = 31
HIST = CONV_KERNEL - 1
N_GROUPS = 4
EXPERTS_PER_GROUP = 8
N_EXPERTS = N_GROUPS * EXPERTS_PER_GROUP
D_EXPERT = D_MODEL // 4
PLE_DIM = 256
PAGE_SIZE = 128
EPS = 1e-6

LANES = 128
SUBLANES = 8
VMEM_LIMIT = 56 * 1024 * 1024

MIX_TM = 512
MIX_TN = 512
ATT_T = 256
ATT_HEADS = 8
PAGE_GROUP = 4
PAGE_SLOTS = 3
CONV_TT = 256
HALO = 32
TOK_TM = 256
ROW_TM = 256
DMA_UNROLL = 8


def _cparams(sem):
    return pltpu.CompilerParams(dimension_semantics=sem, vmem_limit_bytes=VMEM_LIMIT)


def _log_survival(z):
    nz = -z
    return jnp.minimum(nz, 0.0) - jnp.log(1.0 + jnp.exp(jnp.minimum(z, nz)))


def _later_matrix(n):
    row = lax.broadcasted_iota(I32, (n, n), 0)
    col = lax.broadcasted_iota(I32, (n, n), 1)
    return (row > col).astype(BF16)


def _split_dot(l, m):
    hi = l.astype(BF16)
    lo = (l - hi.astype(F32)).astype(BF16)
    return (jnp.dot(hi, m, preferred_element_type=F32)
            + jnp.dot(lo, m, preferred_element_type=F32))


def _mix_kernel(x_ref, g_ref, wa_ref, wb_ref, gq_ref, gk_ref,
                q_ref, k_ref, v_ref, kb_ref, vb_ref, u_ref, xn_ref):
    j = pl.program_id(1)

    @pl.when(j == 0)
    def _():
        x = x_ref[...]
        ms = jnp.mean(x * x, axis=-1, keepdims=True)
        xn_ref[...] = (x * lax.rsqrt(ms + EPS) * g_ref[...]).astype(BF16)

    a = xn_ref[...]
    z = jnp.dot(a, wa_ref[...], preferred_element_type=F32)

    def head_norm(g):
        outs = []
        for hh in range(MIX_TN // HEAD_DIM):
            zh = z[:, hh * HEAD_DIM:(hh + 1) * HEAD_DIM]
            ms = jnp.mean(zh * zh, axis=-1, keepdims=True)
            outs.append(zh * lax.rsqrt(ms + EPS) * g)
        return jnp.concatenate(outs, axis=-1)

    @pl.when(j < 2)
    def _():
        q_ref[...] = (head_norm(gq_ref[...]) * (HEAD_DIM ** -0.5)).astype(BF16)

    @pl.when((j >= 2) & (j < 4))
    def _():
        kn = head_norm(gk_ref[...])
        k_ref[...] = kn
        kb_ref[...] = kn.astype(BF16)

    @pl.when((j >= 4) & (j < 6))
    def _():
        v_ref[...] = z
        vb_ref[...] = z.astype(BF16)

    @pl.when(j >= 6)
    def _():
        zb = jnp.dot(a, wb_ref[...], preferred_element_type=F32)
        u_ref[...] = z * jax.nn.sigmoid(zb)


def _mix_in(x, g_norm, w_in_bf, g_q, g_k):
    t = x.shape[0]
    tm = min(MIX_TM, t)
    nt = t // tm
    half = ATTN_WIDTH // MIX_TN

    def part_map(p):
        return lambda i, j: (i, jnp.clip(j - p * half, 0, half - 1))

    f32_out = jax.ShapeDtypeStruct((t, ATTN_WIDTH), F32)
    bf_out = jax.ShapeDtypeStruct((t, ATTN_WIDTH), BF16)
    out_block = lambda p: pl.BlockSpec((tm, MIX_TN), part_map(p))
    return pl.pallas_call(
        _mix_kernel,
        grid=(nt, 4 * half),
        in_specs=[
            pl.BlockSpec((tm, D_MODEL), lambda i, j: (i, 0)),
            pl.BlockSpec((1, D_MODEL), lambda i, j: (0, 0)),
            pl.BlockSpec((D_MODEL, MIX_TN), lambda i, j: (0, j)),
            pl.BlockSpec((D_MODEL, MIX_TN), lambda i, j: (0, jnp.maximum(j, 3 * half) + half)),
            pl.BlockSpec((1, HEAD_DIM), lambda i, j: (0, 0)),
            pl.BlockSpec((1, HEAD_DIM), lambda i, j: (0, 0)),
        ],
        out_specs=[out_block(0), out_block(1), out_block(2), out_block(1), out_block(2), out_block(3)],
        out_shape=[bf_out, f32_out, f32_out, bf_out, bf_out, f32_out],
        scratch_shapes=[pltpu.VMEM((tm, D_MODEL), BF16)],
        compiler_params=_cparams(("arbitrary", "arbitrary")),
        name="mix_in",
    )(x, g_norm, w_in_bf, w_in_bf, g_q, g_k)


def _attn_prompt_kernel(bias_ref, q_ref, k_ref, v_ref, go_ref, o_ref):
    hg = pl.program_id(1)
    i = pl.program_id(2)
    later = _later_matrix(ATT_T)
    heads = range(ATT_HEADS)
    hs = [slice(n * HEAD_DIM, (n + 1) * HEAD_DIM) for n in heads]
    bias = [bias_ref[hg * ATT_HEADS + n] for n in heads]
    qs = [q_ref[:, hs[n]] for n in heads]
    causal = (lax.broadcasted_iota(I32, (ATT_T, ATT_T), 1) < lax.broadcasted_iota(I32, (ATT_T, ATT_T), 0))

    def block(kb, carry, masked):
        ks = pl.multiple_of(kb * ATT_T, ATT_T)
        zs = [lax.dot_general(qs[n], k_ref[pl.ds(ks, ATT_T), hs[n]], (((1,), (1,)), ((), ())),
                              preferred_element_type=F32) + bias[n] for n in heads]
        ls = [_log_survival(z) for z in zs]
        lms = [jnp.where(causal, l, 0.0) for l in ls] if masked else ls
        rs = [_split_dot(lm, later) for lm in lms]
        avs = []
        for n in heads:
            a = jnp.exp(zs[n] + ls[n] + rs[n])
            if masked:
                a = jnp.where(causal, a, 0.0)
            avs.append(jnp.dot(a.astype(BF16), v_ref[pl.ds(ks, ATT_T), hs[n]], preferred_element_type=F32))
        return tuple((carry[n][0] + jnp.exp(carry[n][1]) * avs[n],
                      carry[n][1] + rs[n][:, 0:1] + lms[n][:, 0:1]) for n in heads)

    carry = tuple((jnp.zeros((ATT_T, HEAD_DIM), F32), jnp.zeros((ATT_T, 1), F32)) for _ in heads)
    carry = block(i, carry, True)
    carry = lax.fori_loop(0, i, lambda t, cr: block(i - 1 - t, cr, False), carry)
    for n in heads:
        o_acc = carry[n][0]
        ms = jnp.mean(o_acc * o_acc, axis=-1, keepdims=True)
        o_ref[:, hs[n]] = (o_acc * lax.rsqrt(ms + EPS) * go_ref[:, hs[n]]).astype(BF16)


def _attn_prompt(sb_bias, q_bf, kb, vb, g_o_row, batch, seq):
    nq = seq // ATT_T
    width = ATT_HEADS * HEAD_DIM
    return pl.pallas_call(
        _attn_prompt_kernel,
        grid=(batch, N_HEADS // ATT_HEADS, nq),
        in_specs=[
            pl.BlockSpec(memory_space=pltpu.SMEM),
            pl.BlockSpec((ATT_T, width), lambda b, h, i: (b * nq + i, h)),
            pl.BlockSpec((seq, width), lambda b, h, i: (b, h)),
            pl.BlockSpec((seq, width), lambda b, h, i: (b, h)),
            pl.BlockSpec((1, width), lambda b, h, i: (0, h)),
        ],
        out_specs=pl.BlockSpec((ATT_T, width), lambda b, h, i: (b * nq + i, h)),
        out_shape=jax.ShapeDtypeStruct((batch * seq, ATTN_WIDTH), BF16),
        compiler_params=_cparams(("arbitrary", "arbitrary", "arbitrary")),
        name="attn_prompt",
    )(sb_bias, q_bf, kb, vb, g_o_row)


def _attn_sample_kernel(pt_ref, q_ref, kn_ref, vn_ref, bias_ref, go_ref, ck_hbm, cv_hbm,
                        o_ref, kbuf, vbuf, sem, *, n_pages, dec_seq, n_seq):
    b = pl.program_id(0)
    rows = N_HEADS * dec_seq
    n_groups = n_pages // PAGE_GROUP

    def group_copies(seq, grp, slot):
        cps = []
        for g in range(PAGE_GROUP):
            page = pt_ref[seq * n_pages + n_pages - 1 - (grp * PAGE_GROUP + g)]
            cps.append(pltpu.make_async_copy(ck_hbm.at[page], kbuf.at[slot, g], sem.at[0, slot, g]))
            cps.append(pltpu.make_async_copy(cv_hbm.at[page], vbuf.at[slot, g], sem.at[1, slot, g]))
        return cps

    def start_group(n):
        @pl.when(n < n_seq * n_groups)
        def _():
            for cp in group_copies(n // n_groups, n % n_groups, n % PAGE_SLOTS):
                cp.start()

    @pl.when(b == 0)
    def _():
        for n in range(PAGE_SLOTS - 1):
            start_group(n)

    heads = range(N_HEADS)
    hs = [slice(h * HEAD_DIM, (h + 1) * HEAD_DIM) for h in heads]
    qr = [slice(h * dec_seq, (h + 1) * dec_seq) for h in heads]
    q = q_ref[0]
    q_h = [q[:, hs[h]].astype(BF16) for h in heads]
    later = _later_matrix(PAGE_SIZE)
    bias = bias_ref[...]

    def blocks(k_of, v_of, n_blocks, carry, mask):
        blks = range(n_blocks)
        zs = [jnp.concatenate(
            [lax.dot_general(q_h[h], k_of(blk, h).astype(BF16), (((1,), (1,)), ((), ())),
                             preferred_element_type=F32) for h in heads], axis=0) + bias for blk in blks]
        ls = [_log_survival(z) for z in zs]
        lms = ls if mask is None else [jnp.where(mask, l, 0.0) for l in ls]
        rs = [_split_dot(lm, later) for lm in lms]
        o_acc, c = carry
        for blk in blks:
            a = jnp.exp(zs[blk] + ls[blk] + rs[blk])
            if mask is not None:
                a = jnp.where(mask, a, 0.0)
            o_blk = jnp.concatenate(
                [jnp.dot(a[qr[h]].astype(BF16), v_of(blk, h).astype(BF16), preferred_element_type=F32)
                 for h in heads], axis=0)
            o_acc = o_acc + jnp.exp(c) * o_blk
            c = c + rs[blk][:, 0:1] + lms[blk][:, 0:1]
        return o_acc, c

    pad = jnp.zeros((PAGE_SIZE - dec_seq, HEAD_DIM), F32)
    k_new, v_new = kn_ref[0], vn_ref[0]
    qi = lax.broadcasted_iota(I32, (rows, PAGE_SIZE), 0) % dec_seq
    kj = lax.broadcasted_iota(I32, (rows, PAGE_SIZE), 1)
    carry = (jnp.zeros((rows, HEAD_DIM), F32), jnp.zeros((rows, 1), F32))
    carry = blocks(lambda blk, h: jnp.concatenate([k_new[:, hs[h]], pad], axis=0),
                   lambda blk, h: jnp.concatenate([v_new[:, hs[h]], pad], axis=0), 1, carry, kj < qi)

    def group_step(t, carry):
        n = b * n_groups + t
        slot = n % PAGE_SLOTS
        for cp in group_copies(0, 0, slot):
            cp.wait()
        start_group(n + PAGE_SLOTS - 1)

        def head_rows(buf):
            return lambda g, h: buf[slot, g, pl.ds(h, PAGE_SIZE, stride=N_HEADS), :]
        return blocks(head_rows(kbuf), head_rows(vbuf), PAGE_GROUP, carry, None)

    o_acc, _ = lax.fori_loop(0, n_groups, group_step, carry)
    ms = jnp.mean(o_acc * o_acc, axis=-1, keepdims=True)
    o_ref[0] = o_acc * lax.rsqrt(ms + EPS) * go_ref[...]


def _attn_sample(page_ids, q_s, k_s, v_s, bias_rows, go_rows, cache_k, cache_v):
    nb, dec_seq, _ = q_s.shape
    n_pages = page_ids.shape[1]
    rows = N_HEADS * dec_seq
    seq_block = pl.BlockSpec((1, dec_seq, ATTN_WIDTH), lambda b, pt: (b, 0, 0))
    grid_spec = pltpu.PrefetchScalarGridSpec(
        num_scalar_prefetch=1,
        grid=(nb,),
        in_specs=[
            seq_block, seq_block, seq_block,
            pl.BlockSpec((rows, PAGE_SIZE), lambda b, pt: (0, 0)),
            pl.BlockSpec((rows, HEAD_DIM), lambda b, pt: (0, 0)),
            pl.BlockSpec(memory_space=pl.ANY),
            pl.BlockSpec(memory_space=pl.ANY),
        ],
        out_specs=pl.BlockSpec((1, rows, HEAD_DIM), lambda b, pt: (b, 0, 0)),
        scratch_shapes=[
            pltpu.VMEM((PAGE_SLOTS, PAGE_GROUP, PAGE_SIZE * N_HEADS, HEAD_DIM), F32),
            pltpu.VMEM((PAGE_SLOTS, PAGE_GROUP, PAGE_SIZE * N_HEADS, HEAD_DIM), F32),
            pltpu.SemaphoreType.DMA((2, PAGE_SLOTS, PAGE_GROUP)),
        ],
    )
    return pl.pallas_call(
        functools.partial(_attn_sample_kernel, n_pages=n_pages, dec_seq=dec_seq, n_seq=nb),
        grid_spec=grid_spec,
        out_shape=jax.ShapeDtypeStruct((nb, rows, HEAD_DIM), F32),
        compiler_params=_cparams(("arbitrary",)),
        name="attn_sample",
    )(page_ids.reshape(-1), q_s, k_s, v_s, bias_rows, go_rows, cache_k, cache_v)


def _conv_kernel(u_ref, hist_ref, w_ref, b_ref, g_ref, lb_ref, beta_ref, y_ref, win_ref, acc_ref, shift_ref,
                 *, tt):
    i = pl.program_id(1)

    @pl.when(i == 0)
    def _():
        win_ref[0:HALO, :] = hist_ref[0]

    @pl.when(i > 0)
    def _():
        win_ref[0:HALO, :] = win_ref[tt:tt + HALO, :]

    win_ref[HALO:HALO + tt, :] = u_ref[0]
    off = HALO - HIST
    def lane_block(cb, _):
        cs = pl.ds(pl.multiple_of(cb * LANES, LANES), LANES)
        acc = jnp.zeros((tt, LANES), F32)
        for s in range(SUBLANES):
            taps = [k for k in range(CONV_KERNEL) if (off + k) % SUBLANES == s]
            span = max(off + k for k in taps) - s + tt
            shift_ref[0:span, :] = win_ref[s:s + span, cs]
            for k in taps:
                base = off + k - s
                acc = acc + w_ref[k:k + 1, cs] * shift_ref[base:base + tt, :]
        acc_ref[:, cs] = acc
        return 0
    lax.fori_loop(0, CONV_CH // LANES, lane_block, 0)
    y = acc_ref[...] + b_ref[...]
    mu = jnp.mean(y, axis=-1, keepdims=True)
    yc = y - mu
    var = jnp.mean(yc * yc, axis=-1, keepdims=True)
    y = yc * lax.rsqrt(var + EPS) * g_ref[...] + lb_ref[...]
    y_ref[0] = (y * jax.nn.sigmoid(y) * beta_ref[...]).astype(BF16)


def _conv(u, hist, w_dw, b_dw, ln_g, ln_b, beta, tt):
    nb, s, c = u.shape
    row = pl.BlockSpec((1, c), lambda b, i: (0, 0))
    return pl.pallas_call(
        functools.partial(_conv_kernel, tt=tt),
        grid=(nb, s // tt),
        in_specs=[
            pl.BlockSpec((1, tt, c), lambda b, i: (b, i, 0)),
            pl.BlockSpec((1, HALO, c), lambda b, i: (b, 0, 0)),
            pl.BlockSpec((CONV_KERNEL, c), lambda b, i: (0, 0)),
            row, row, row, row,
        ],
        out_specs=pl.BlockSpec((1, tt, c), lambda b, i: (b, i, 0)),
        out_shape=jax.ShapeDtypeStruct((nb, s, c), BF16),
        scratch_shapes=[pltpu.VMEM((HALO + tt, c), F32), pltpu.VMEM((tt, c), F32),
                        pltpu.VMEM((HALO + tt, LANES), F32)],
        compiler_params=_cparams(("arbitrary", "arbitrary")),
        name="conv",
    )(u, hist, w_dw, b_dw, ln_g, ln_b, beta)


R_E1, R_E2, R_W1, R_W2, R_RANK1, R_RANK2 = range(6)
GRP_LANE0 = N_EXPERTS


def _lane_pick(x, lane, idx):
    return jnp.sum(jnp.where(lane == idx, x, 0.0), axis=-1, keepdims=True)


def _post_kernel(hp_ref, hs_ref, op_ref, os_ref, ycp_ref, ycs_ref, wt_ref, wb_ref, g_ref, wr_ref, br_ref,
                 h1_ref, xn_ref, route_ref, tot_ref, run_ref, *, n_prompt_tiles):
    i = pl.program_id(0)

    @pl.when(i == 0)
    def _():
        run_ref[...] = jnp.zeros_like(run_ref)

    is_prompt = i < n_prompt_tiles
    pick = lambda p_ref, s_ref: jnp.where(is_prompt, p_ref[...], s_ref[...])
    h1 = (pick(hp_ref, hs_ref)
          + jnp.dot(pick(op_ref, os_ref), wt_ref[...], preferred_element_type=F32)
          + jnp.dot(pick(ycp_ref, ycs_ref), wb_ref[...], preferred_element_type=F32))
    h1_ref[...] = h1
    ms = jnp.mean(h1 * h1, axis=-1, keepdims=True)
    xn = h1 * lax.rsqrt(ms + EPS) * g_ref[...]
    xn_ref[...] = xn

    x_hi = xn.astype(BF16)
    x_lo = (xn - x_hi.astype(F32)).astype(BF16)
    logits = (jnp.dot(x_hi, wr_ref[0], preferred_element_type=F32)
              + jnp.dot(x_lo, wr_ref[0], preferred_element_type=F32)
              + jnp.dot(x_hi, wr_ref[1], preferred_element_type=F32)) + br_ref[...]
    tm = logits.shape[0]
    lane = lax.broadcasted_iota(I32, (tm, LANES), 1)
    neg = -jnp.inf
    is_grp = (lane >= GRP_LANE0) & (lane < GRP_LANE0 + N_GROUPS)
    glog = jnp.where(is_grp, logits, neg)
    gmax = jnp.max(glog, axis=-1, keepdims=True)
    gidx = jnp.min(jnp.where(glog == gmax, lane - GRP_LANE0, LANES), axis=-1, keepdims=True)
    g_w = 1.0 / jnp.sum(jnp.exp(glog - gmax), axis=-1, keepdims=True)

    in_grp = (lane < N_EXPERTS) & (lane // EXPERTS_PER_GROUP == gidx)
    elog = jnp.where(in_grp, logits, neg)
    v1 = jnp.max(elog, axis=-1, keepdims=True)
    i1 = jnp.min(jnp.where(elog == v1, lane, LANES), axis=-1, keepdims=True)
    elog2 = jnp.where(lane == i1, neg, elog)
    v2 = jnp.max(elog2, axis=-1, keepdims=True)
    i2 = jnp.min(jnp.where(elog2 == v2, lane, LANES), axis=-1, keepdims=True)
    ew = jnp.exp(v2 - v1)
    w1 = g_w / (1.0 + ew)
    w2 = g_w * ew / (1.0 + ew)

    onehot = ((lane == i1) | (lane == i2)).astype(F32)
    trow = lax.broadcasted_iota(I32, (tm, tm), 0)
    tcol = lax.broadcasted_iota(I32, (tm, tm), 1)
    earlier = (tcol < trow).astype(BF16)
    before = run_ref[0:1, :] + jnp.dot(earlier, onehot.astype(BF16), preferred_element_type=F32)
    rank1 = _lane_pick(before, lane, i1)
    rank2 = _lane_pick(before, lane, i2)
    run_ref[...] = run_ref[...] + jnp.sum(onehot, axis=0, keepdims=True)
    tot_ref[...] = run_ref[...]

    rec = jnp.zeros((tm, LANES), F32)
    for idx, val in ((R_E1, i1.astype(F32)), (R_E2, i2.astype(F32)), (R_W1, w1), (R_W2, w2),
                     (R_RANK1, rank1), (R_RANK2, rank2)):
        rec = jnp.where(lane == idx, val, rec)
    route_ref[...] = rec


def _post(h_p, h_s, o_p, o_s, yc_p, yc_s, w_top, w_bot, g_ffn, w_router, b_router):
    npt = h_p.shape[0] // TOK_TM
    nst = h_s.shape[0] // TOK_TM
    t = (npt + nst) * TOK_TM
    tok = lambda w: pl.BlockSpec((TOK_TM, w), lambda i: (i, 0))
    full = lambda a: pl.BlockSpec(a.shape, lambda i: (0,) * a.ndim)
    prompt_tile = lambda w: pl.BlockSpec((TOK_TM, w), lambda i: (jnp.minimum(i, npt - 1), 0))
    sample_tile = lambda w: pl.BlockSpec((TOK_TM, w), lambda i: (jnp.clip(i - npt, 0, nst - 1), 0))
    return pl.pallas_call(
        functools.partial(_post_kernel, n_prompt_tiles=npt),
        grid=(npt + nst,),
        in_specs=[prompt_tile(D_MODEL), sample_tile(D_MODEL), prompt_tile(ATTN_WIDTH), sample_tile(ATTN_WIDTH),
                  prompt_tile(CONV_CH), sample_tile(CONV_CH), full(w_top), full(w_bot),
                  full(g_ffn), full(w_router), full(b_router)],
        out_specs=[tok(D_MODEL), tok(D_MODEL), tok(LANES), pl.BlockSpec((8, LANES), lambda i: (0, 0))],
        out_shape=[jax.ShapeDtypeStruct((t, D_MODEL), F32), jax.ShapeDtypeStruct((t, D_MODEL), F32),
                   jax.ShapeDtypeStruct((t, LANES), F32), jax.ShapeDtypeStruct((8, LANES), F32)],
        scratch_shapes=[pltpu.VMEM((8, LANES), F32)],
        compiler_params=_cparams(("arbitrary",)),
        name="post",
    )(h_p, h_s, o_p, o_s, yc_p, yc_s, w_top, w_bot, g_ffn, w_router, b_router)


M_TILE_EXPERT, M_NUM_TILES = 0, 1


def _positions_kernel(route_ref, tot_ref, pos_ref, meta_ref):
    t = route_ref.shape[0]
    lane1 = lax.broadcasted_iota(I32, (8, LANES), 1)
    total = tot_ref[...]
    ntile = jnp.where(lane1 < N_EXPERTS, jnp.ceil(total / ROW_TM), 0.0)
    srow = lax.broadcasted_iota(I32, (LANES, LANES), 0)
    scol = lax.broadcasted_iota(I32, (LANES, LANES), 1)
    start_tile = jnp.dot(ntile.astype(BF16), (srow < scol).astype(BF16), preferred_element_type=F32)
    start_row = start_tile[0:1, :] * ROW_TM

    route = route_ref[...]
    lane = lax.broadcasted_iota(I32, (t, LANES), 1)
    e1 = _lane_pick(route, lane, R_E1).astype(I32)
    e2 = _lane_pick(route, lane, R_E2).astype(I32)
    p1 = _lane_pick(jnp.broadcast_to(start_row, (t, LANES)), lane, e1) + _lane_pick(route, lane, R_RANK1)
    p2 = _lane_pick(jnp.broadcast_to(start_row, (t, LANES)), lane, e2) + _lane_pick(route, lane, R_RANK2)
    pos_ref[...] = jnp.where(lane == 0, p1, jnp.where(lane == 1, p2, 0.0)).astype(I32)

    end_tile = start_tile[0:1, :] + ntile[0:1, :]
    end_col = jnp.sum(jnp.where(srow == scol, jnp.broadcast_to(end_tile, (LANES, LANES)), 0.0),
                      axis=-1, keepdims=True)
    done = ((end_col <= scol.astype(F32)) & (srow < N_EXPERTS)).astype(F32)
    tile_expert = jnp.minimum(jnp.sum(done, axis=0, keepdims=True), N_EXPERTS - 1.0)
    num_tiles = jnp.sum(ntile[0:1, :], axis=-1, keepdims=True)
    row8 = lax.broadcasted_iota(I32, (8, LANES), 0)
    meta = jnp.where(row8 == M_TILE_EXPERT, jnp.broadcast_to(tile_expert, (8, LANES)),
                     jnp.where(row8 == M_NUM_TILES, jnp.broadcast_to(num_tiles, (8, LANES)), 0.0))
    meta_ref[...] = meta.astype(I32)


def _positions(route, totals):
    t = route.shape[0]
    return pl.pallas_call(
        _positions_kernel,
        out_shape=[jax.ShapeDtypeStruct((t, LANES), I32), jax.ShapeDtypeStruct((8, LANES), I32)],
        compiler_params=pltpu.CompilerParams(vmem_limit_bytes=VMEM_LIMIT),
        name="positions",
    )(route, totals)


def _experts_kernel(te_ref, nt_ref, p1_ref, p2_ref, xn_hbm, wg_ref, wu_ref, wd_ref,
                    y_ref, row_tok, xg, wg_bf, wu_bf, wd_bf, sem, *, n_tok, n_rows):
    j = pl.program_id(0)
    n_tiles = nt_ref[0]

    def row_copy(r, slot, tok):
        return pltpu.make_async_copy(xn_hbm.at[pl.ds(tok, 1), :], xg.at[slot, pl.ds(r, 1), :], sem.at[slot])

    def gather(tile, slot):
        def issue(r, _):
            row_copy(r, slot, row_tok[tile * ROW_TM + r]).start()
            return 0
        lax.fori_loop(0, ROW_TM, issue, 0, unroll=DMA_UNROLL)

    @pl.when(j == 0)
    def _():
        def clear(r, _):
            row_tok[r] = 0
            return 0
        lax.fori_loop(0, n_rows, clear, 0, unroll=DMA_UNROLL)

        def place(t, _):
            row_tok[p1_ref[t]] = t
            row_tok[p2_ref[t]] = t
            return 0
        lax.fori_loop(0, n_tok, place, 0, unroll=DMA_UNROLL)
        gather(0, 0)

    @pl.when(j < n_tiles)
    def _():
        slot = j % 2

        @pl.when(j + 1 < n_tiles)
        def _():
            gather(j + 1, 1 - slot)

        new_expert = (j == 0) | (te_ref[j] != te_ref[jnp.maximum(j - 1, 0)])

        @pl.when(new_expert)
        def _():
            wg_bf[...] = wg_ref[0].astype(BF16)
            wu_bf[...] = wu_ref[0].astype(BF16)
            wd_bf[...] = wd_ref[0].astype(BF16)

        def drain(r, _):
            row_copy(r, slot, 0).wait()
            return 0
        lax.fori_loop(0, ROW_TM, drain, 0, unroll=DMA_UNROLL)

        x = xg[slot].astype(BF16)
        hg = jnp.dot(x, wg_bf[...], preferred_element_type=F32)
        hu = jnp.dot(x, wu_bf[...], preferred_element_type=F32)
        hid = (hg * jax.nn.sigmoid(hg) * hu).astype(BF16)
        y_ref[...] = jnp.dot(hid, wd_bf[...], preferred_element_type=F32)

    @pl.when(j >= nt_ref[0])
    def _():
        y_ref[...] = jnp.zeros_like(y_ref)


def _experts(tile_expert, num_tiles, pos1, pos2, xn, w_gate, w_up, w_down):
    n_tok = xn.shape[0]
    max_tiles = (2 * n_tok + N_EXPERTS * (ROW_TM - 1)) // ROW_TM
    n_rows = max_tiles * ROW_TM
    w_in_spec = pl.BlockSpec((1, D_MODEL, D_EXPERT), lambda j, te, nt, p1, p2: (te[j], 0, 0))
    grid_spec = pltpu.PrefetchScalarGridSpec(
        num_scalar_prefetch=4,
        grid=(max_tiles,),
        in_specs=[
            pl.BlockSpec(memory_space=pl.ANY),
            w_in_spec, w_in_spec,
            pl.BlockSpec((1, D_EXPERT, D_MODEL), lambda j, te, nt, p1, p2: (te[j], 0, 0)),
        ],
        out_specs=pl.BlockSpec((ROW_TM, D_MODEL), lambda j, te, nt, p1, p2: (j, 0)),
        scratch_shapes=[
            pltpu.SMEM((n_rows,), I32),
            pltpu.VMEM((2, ROW_TM, D_MODEL), F32),
            pltpu.VMEM((D_MODEL, D_EXPERT), BF16),
            pltpu.VMEM((D_MODEL, D_EXPERT), BF16),
            pltpu.VMEM((D_EXPERT, D_MODEL), BF16),
            pltpu.SemaphoreType.DMA((2,)),
        ],
    )
    return pl.pallas_call(
        functools.partial(_experts_kernel, n_tok=n_tok, n_rows=n_rows),
        grid_spec=grid_spec,
        out_shape=jax.ShapeDtypeStruct((n_rows, D_MODEL), F32),
        compiler_params=_cparams(("arbitrary",)),
        name="experts",
    )(tile_expert, num_tiles, pos1, pos2, xn, w_gate, w_up, w_down)


def _final_kernel(p1_ref, p2_ref, h1_ref, route_ref, pp_ref, ps_ref, ys_hbm, g_ref, wpg_ref, wpp_ref,
                  outp_ref, outs_ref, y1, y2, sem, *, n_prompt_tiles):
    i = pl.program_id(0)
    slot = i % 2

    def row_copies(r, slot, pos1, pos2):
        return (pltpu.make_async_copy(ys_hbm.at[pl.ds(pos1, 1), :], y1.at[slot, pl.ds(r, 1), :], sem.at[0, slot]),
                pltpu.make_async_copy(ys_hbm.at[pl.ds(pos2, 1), :], y2.at[slot, pl.ds(r, 1), :], sem.at[1, slot]))

    def gather(tile, slot):
        def issue(r, _):
            tok = tile * TOK_TM + r
            for cp in row_copies(r, slot, p1_ref[tok], p2_ref[tok]):
                cp.start()
            return 0
        lax.fori_loop(0, TOK_TM, issue, 0, unroll=DMA_UNROLL)

    def drain(slot):
        def wait_row(r, _):
            for cp in row_copies(r, slot, 0, 0):
                cp.wait()
            return 0
        lax.fori_loop(0, TOK_TM, wait_row, 0, unroll=DMA_UNROLL)

    @pl.when(i == 0)
    def _():
        gather(0, 0)

    @pl.when(i + 1 < pl.num_programs(0))
    def _():
        gather(i + 1, 1 - slot)

    route = route_ref[...]
    lane = lax.broadcasted_iota(I32, route.shape, 1)
    w1 = _lane_pick(route, lane, R_W1)
    w2 = _lane_pick(route, lane, R_W2)
    p = jnp.where(i < n_prompt_tiles, pp_ref[...], ps_ref[...])
    emb = jnp.dot(p.astype(BF16), wpp_ref[...], preferred_element_type=F32)

    drain(slot)
    h2 = h1_ref[...] + w1 * y1[slot] + w2 * y2[slot]
    ms = jnp.mean(h2 * h2, axis=-1, keepdims=True)
    hn = (h2 * lax.rsqrt(ms + EPS) * g_ref[...]).astype(BF16)
    gate = jax.nn.sigmoid(jnp.dot(hn, wpg_ref[...], preferred_element_type=F32))
    res = h2 + gate * emb

    @pl.when(i < n_prompt_tiles)
    def _():
        outp_ref[...] = res

    @pl.when(i >= n_prompt_tiles)
    def _():
        outs_ref[...] = res


def _final(pos1, pos2, h1, route, p_p, p_s, ys, g_ple, w_ple_gate_bf, w_ple_proj_bf):
    t = h1.shape[0]
    n_prompt = p_p.shape[0]
    npt = n_prompt // TOK_TM
    nst = (t - n_prompt) // TOK_TM
    tok = lambda w: pl.BlockSpec((TOK_TM, w), lambda i, p1, p2: (i, 0))
    full = lambda a: pl.BlockSpec(a.shape, lambda i, p1, p2: (0,) * a.ndim)
    prompt_tile = lambda w: pl.BlockSpec((TOK_TM, w), lambda i, p1, p2: (jnp.minimum(i, npt - 1), 0))
    sample_tile = lambda w: pl.BlockSpec((TOK_TM, w), lambda i, p1, p2: (jnp.clip(i - npt, 0, nst - 1), 0))
    grid_spec = pltpu.PrefetchScalarGridSpec(
        num_scalar_prefetch=2,
        grid=(t // TOK_TM,),
        in_specs=[tok(D_MODEL), tok(LANES), prompt_tile(PLE_DIM), sample_tile(PLE_DIM),
                  pl.BlockSpec(memory_space=pl.ANY),
                  full(g_ple), full(w_ple_gate_bf), full(w_ple_proj_bf)],
        out_specs=[prompt_tile(D_MODEL), sample_tile(D_MODEL)],
        scratch_shapes=[pltpu.VMEM((2, TOK_TM, D_MODEL), F32), pltpu.VMEM((2, TOK_TM, D_MODEL), F32),
                        pltpu.SemaphoreType.DMA((2, 2))],
    )
    return pl.pallas_call(
        functools.partial(_final_kernel, n_prompt_tiles=npt),
        grid_spec=grid_spec,
        out_shape=[jax.ShapeDtypeStruct((n_prompt, D_MODEL), F32),
                   jax.ShapeDtypeStruct((t - n_prompt, D_MODEL), F32)],
        compiler_params=_cparams(("arbitrary",)),
        name="final",
    )(pos1, pos2, h1, route, p_p, p_s, ys, g_ple, w_ple_gate_bf, w_ple_proj_bf)


def _layer(i, hp, hs, cache_k, cache_v, state_conv, page_table, p_prompt, p_sample,
           g_mix_norm, w_in, g_q, g_k, sb_bias, g_o, w_dw, b_dw, ln_g, ln_b, beta_conv, w_out,
           g_ffn_norm, w_grp, b_grp, w_rt, b_rt, w_gate, w_up, w_down,
           g_ple_norm, w_ple_proj, w_ple_gate):
    batch, seq, _ = hp.shape
    nb, dec_seq, _ = hs.shape
    n_prompt = batch * seq
    n_sample = nb * dec_seq
    row = lambda a: a.reshape(1, -1)

    x_p = hp.reshape(n_prompt, D_MODEL)
    x_s = hs.reshape(n_sample, D_MODEL)
    mix_args = (row(g_mix_norm[i]), w_in[i].astype(BF16), row(g_q[i]), row(g_k[i]))
    q_p, k_p, v_p, kb_p, vb_p, u_p = _mix_in(x_p, *mix_args)
    q_s, k_s, v_s, _, _, u_s = _mix_in(x_s, *mix_args)

    o_p = _attn_prompt(sb_bias[i], q_p, kb_p, vb_p, row(g_o[i]), batch, seq)

    rows = N_HEADS * dec_seq
    split = lambda a: a.astype(F32).reshape(nb, dec_seq, ATTN_WIDTH)
    bias_rows = jnp.broadcast_to(jnp.repeat(sb_bias[i], dec_seq)[:, None], (rows, PAGE_SIZE))
    go_rows = jnp.repeat(g_o[i], dec_seq, axis=0)
    depth, n_pool = cache_k.shape[:2]
    o_s = _attn_sample(page_table + i * n_pool, split(q_s), split(k_s), split(v_s), bias_rows, go_rows,
                       cache_k.reshape(depth * n_pool, PAGE_SIZE * N_HEADS, HEAD_DIM),
                       cache_v.reshape(depth * n_pool, PAGE_SIZE * N_HEADS, HEAD_DIM))
    o_s = o_s.reshape(nb, N_HEADS, dec_seq, HEAD_DIM).transpose(0, 2, 1, 3).reshape(n_sample, ATTN_WIDTH)

    conv_args = (w_dw[i], row(b_dw[i]), row(ln_g[i]), row(ln_b[i]), row(beta_conv[i]))
    u_p = u_p.reshape(batch, seq, CONV_CH)
    u_s = u_s.reshape(nb, dec_seq, CONV_CH)
    hist_p = jnp.zeros((batch, HALO, CONV_CH), F32)
    hist_s = jnp.pad(state_conv[i], ((0, 0), (HALO - HIST, 0), (0, 0)))
    yc_p = _conv(u_p, hist_p, *conv_args, tt=CONV_TT)
    yc_s = _conv(u_s, hist_s, *conv_args, tt=dec_seq)

    w_out_bf = w_out[i].astype(BF16)
    w_router = jnp.zeros((D_MODEL, LANES), F32)
    w_router = w_router.at[:, :N_EXPERTS].set(w_rt[i]).at[:, GRP_LANE0:GRP_LANE0 + N_GROUPS].set(w_grp[i])
    b_router = jnp.zeros((1, LANES), F32)
    b_router = b_router.at[0, :N_EXPERTS].set(b_rt[i]).at[0, GRP_LANE0:GRP_LANE0 + N_GROUPS].set(b_grp[i])
    w_router_hi = w_router.astype(BF16)
    w_router_lo = (w_router - w_router_hi.astype(F32)).astype(BF16)
    h1, xn, route, totals = _post(x_p, x_s, o_p, o_s.astype(BF16), yc_p.reshape(n_prompt, CONV_CH),
                                  yc_s.reshape(n_sample, CONV_CH), w_out_bf[:ATTN_WIDTH], w_out_bf[ATTN_WIDTH:],
                                  row(g_ffn_norm[i]), jnp.stack([w_router_hi, w_router_lo]), b_router)

    pos, meta = _positions(route, totals)
    pos1, pos2 = pos[:, 0], pos[:, 1]
    ys = _experts(meta[M_TILE_EXPERT], meta[M_NUM_TILES, :1], pos1, pos2, xn, w_gate[i], w_up[i], w_down[i])

    y_p, y_s = _final(pos1, pos2, h1, route, p_prompt[i].reshape(n_prompt, PLE_DIM),
                      p_sample[i].reshape(n_sample, PLE_DIM), ys, row(g_ple_norm[i]),
                      w_ple_gate[i].astype(BF16), w_ple_proj[i].astype(BF16))

    heads = lambda a, b_, s_: a.reshape(b_, s_, N_HEADS, HEAD_DIM)
    conv_p = u_p[:, seq - HIST:]
    conv_s = jnp.concatenate([state_conv[i], u_s], axis=1)[:, -HIST:]
    return (y_p.reshape(batch, seq, D_MODEL), y_s.reshape(nb, dec_seq, D_MODEL),
            heads(k_p, batch, seq), heads(v_p, batch, seq), conv_p,
            heads(k_s, nb, dec_seq), heads(v_s, nb, dec_seq), conv_s)


def kernel(x_prompt, x_sample, cache_k, cache_v, state_conv, page_table, p_prompt, p_sample, g_mix_norm, w_in, g_q, g_k, sb_bias, g_o, w_dw, b_dw, ln_g, ln_b, beta_conv, w_out, g_ffn_norm, w_grp, b_grp, w_rt, b_rt, w_gate, w_up, w_down, g_ple_norm, w_ple_proj, w_ple_gate):
    depth = w_in.shape[0]
    hp, hs = x_prompt, x_sample
    outs = []
    for i in range(depth):
        res = _layer(i, hp, hs, cache_k, cache_v, state_conv, page_table, p_prompt, p_sample,
                     g_mix_norm, w_in, g_q, g_k, sb_bias, g_o, w_dw, b_dw, ln_g, ln_b, beta_conv, w_out,
                     g_ffn_norm, w_grp, b_grp, w_rt, b_rt, w_gate, w_up, w_down,
                     g_ple_norm, w_ple_proj, w_ple_gate)
        hp, hs = res[0], res[1]
        outs.append(res[2:])
    stacked = tuple(jnp.stack([o[n] for o in outs], axis=0) for n in range(6))
    return (hp, hs) + stacked
```

```python
import functools

import jax
import jax.numpy as jnp
from jax import lax
from jax.experimental import pallas as pl
from jax.experimental.pallas import tpu as pltpu

F32 = jnp.float32
BF16 = jnp.bfloat16
I32 = jnp.int32

D_MODEL = 2048
HEAD_DIM = 128
N_HEADS = 8
ATTN_WIDTH = N_HEADS * HEAD_DIM
CONV_CH = D_MODEL - ATTN_WIDTH
CONV_KERNEL = 31
HIST = CONV_KERNEL - 1
N_GROUPS = 4
EXPERTS_PER_GROUP = 8
N_EXPERTS = N_GROUPS * EXPERTS_PER_GROUP
D_EXPERT = D_MODEL // 4
PLE_DIM = 256
PAGE_SIZE = 128
EPS = 1e-6

LANES = 128
SUBLANES = 8
ROW_SLAB = D_MODEL // LANES
VMEM_LIMIT = 56 * 1024 * 1024

MIX_TM = 1024
MIX_TN = 512
ATT_T = 256
ATT_HEADS = 8
PAGE_GROUP = 4
PAGE_SLOTS = 3
CONV_TT = 256
HALO = 32
TOK_TM = 256
ROW_TM = 256
DMA_UNROLL = 8


def _cparams(sem):
    return pltpu.CompilerParams(dimension_semantics=sem, vmem_limit_bytes=VMEM_LIMIT)


def _log_survival(z):
    nz = -z
    return jnp.minimum(nz, 0.0) - jnp.log(1.0 + jnp.exp(jnp.minimum(z, nz)))


def _later_matrix(n):
    row = lax.broadcasted_iota(I32, (n, n), 0)
    col = lax.broadcasted_iota(I32, (n, n), 1)
    return (row > col).astype(BF16)


def _split_dot(l, m):
    hi = l.astype(BF16)
    lo = (l - hi.astype(F32)).astype(BF16)
    return (jnp.dot(hi, m, preferred_element_type=F32)
            + jnp.dot(lo, m, preferred_element_type=F32))


def _mix_kernel(x_ref, g_ref, wa_ref, wb_ref, gq_ref, gk_ref,
                q_ref, k_ref, v_ref, kb_ref, vb_ref, u_ref, xn_ref):
    j = pl.program_id(1)

    @pl.when(j == 0)
    def _():
        x = x_ref[...]
        ms = jnp.mean(x * x, axis=-1, keepdims=True)
        xn_ref[...] = (x * lax.rsqrt(ms + EPS) * g_ref[...]).astype(BF16)

    a = xn_ref[...]
    z = jnp.dot(a, wa_ref[...], preferred_element_type=F32)

    def head_norm(g):
        outs = []
        for hh in range(MIX_TN // HEAD_DIM):
            zh = z[:, hh * HEAD_DIM:(hh + 1) * HEAD_DIM]
            ms = jnp.mean(zh * zh, axis=-1, keepdims=True)
            outs.append(zh * lax.rsqrt(ms + EPS) * g)
        return jnp.concatenate(outs, axis=-1)

    @pl.when(j < 2)
    def _():
        q_ref[...] = (head_norm(gq_ref[...]) * (HEAD_DIM ** -0.5)).astype(BF16)

    @pl.when((j >= 2) & (j < 4))
    def _():
        kn = head_norm(gk_ref[...])
        k_ref[...] = kn
        kb_ref[...] = kn.astype(BF16)

    @pl.when((j >= 4) & (j < 6))
    def _():
        v_ref[...] = z
        vb_ref[...] = z.astype(BF16)

    @pl.when(j >= 6)
    def _():
        zb = jnp.dot(a, wb_ref[...], preferred_element_type=F32)
        u_ref[...] = z * jax.nn.sigmoid(zb)


def _mix_in(x, g_norm, w_in_bf, g_q, g_k):
    t = x.shape[0]
    tm = min(MIX_TM, t)
    nt = t // tm
    half = ATTN_WIDTH // MIX_TN

    def part_map(p):
        return lambda i, j: (i, jnp.clip(j - p * half, 0, half - 1))

    f32_out = jax.ShapeDtypeStruct((t, ATTN_WIDTH), F32)
    bf_out = jax.ShapeDtypeStruct((t, ATTN_WIDTH), BF16)
    out_block = lambda p: pl.BlockSpec((tm, MIX_TN), part_map(p))
    return pl.pallas_call(
        _mix_kernel,
        grid=(nt, 4 * half),
        in_specs=[
            pl.BlockSpec((tm, D_MODEL), lambda i, j: (i, 0)),
            pl.BlockSpec((1, D_MODEL), lambda i, j: (0, 0)),
            pl.BlockSpec((D_MODEL, MIX_TN), lambda i, j: (0, j)),
            pl.BlockSpec((D_MODEL, MIX_TN), lambda i, j: (0, jnp.maximum(j, 3 * half) + half)),
            pl.BlockSpec((1, HEAD_DIM), lambda i, j: (0, 0)),
            pl.BlockSpec((1, HEAD_DIM), lambda i, j: (0, 0)),
        ],
        out_specs=[out_block(0), out_block(1), out_block(2), out_block(1), out_block(2), out_block(3)],
        out_shape=[bf_out, f32_out, f32_out, bf_out, bf_out, f32_out],
        scratch_shapes=[pltpu.VMEM((tm, D_MODEL), BF16)],
        compiler_params=_cparams(("arbitrary", "arbitrary")),
        name="mix_in",
    )(x, g_norm, w_in_bf, w_in_bf, g_q, g_k)


def _attn_prompt_kernel(bias_ref, q_ref, k_ref, v_ref, go_ref, o_ref):
    hg = pl.program_id(1)
    i = pl.program_id(2)
    later = _later_matrix(ATT_T)
    heads = range(ATT_HEADS)
    hs = [slice(n * HEAD_DIM, (n + 1) * HEAD_DIM) for n in heads]
    bias = [bias_ref[hg * ATT_HEADS + n] for n in heads]
    qs = [q_ref[:, hs[n]] for n in heads]
    causal = (lax.broadcasted_iota(I32, (ATT_T, ATT_T), 1) < lax.broadcasted_iota(I32, (ATT_T, ATT_T), 0))

    def block(kb, carry, masked):
        ks = pl.multiple_of(kb * ATT_T, ATT_T)
        zs = [lax.dot_general(qs[n], k_ref[pl.ds(ks, ATT_T), hs[n]], (((1,), (1,)), ((), ())),
                              preferred_element_type=F32) + bias[n] for n in heads]
        ls = [_log_survival(z) for z in zs]
        lms = [jnp.where(causal, l, 0.0) for l in ls] if masked else ls
        rs = [_split_dot(lm, later) for lm in lms]
        avs = []
        for n in heads:
            a = jnp.exp(zs[n] + ls[n] + rs[n])
            if masked:
                a = jnp.where(causal, a, 0.0)
            avs.append(jnp.dot(a.astype(BF16), v_ref[pl.ds(ks, ATT_T), hs[n]], preferred_element_type=F32))
        return tuple((carry[n][0] + jnp.exp(carry[n][1]) * avs[n],
                      carry[n][1] + rs[n][:, 0:1] + lms[n][:, 0:1]) for n in heads)

    carry = tuple((jnp.zeros((ATT_T, HEAD_DIM), F32), jnp.zeros((ATT_T, 1), F32)) for _ in heads)
    carry = block(i, carry, True)
    carry = lax.fori_loop(0, i, lambda t, cr: block(i - 1 - t, cr, False), carry)
    for n in heads:
        o_acc = carry[n][0]
        ms = jnp.mean(o_acc * o_acc, axis=-1, keepdims=True)
        o_ref[:, hs[n]] = (o_acc * lax.rsqrt(ms + EPS) * go_ref[:, hs[n]]).astype(BF16)


def _attn_prompt(sb_bias, q_bf, kb, vb, g_o_row, batch, seq):
    nq = seq // ATT_T
    width = ATT_HEADS * HEAD_DIM
    return pl.pallas_call(
        _attn_prompt_kernel,
        grid=(batch, N_HEADS // ATT_HEADS, nq),
        in_specs=[
            pl.BlockSpec(memory_space=pltpu.SMEM),
            pl.BlockSpec((ATT_T, width), lambda b, h, i: (b * nq + i, h)),
            pl.BlockSpec((seq, width), lambda b, h, i: (b, h)),
            pl.BlockSpec((seq, width), lambda b, h, i: (b, h)),
            pl.BlockSpec((1, width), lambda b, h, i: (0, h)),
        ],
        out_specs=pl.BlockSpec((ATT_T, width), lambda b, h, i: (b * nq + i, h)),
        out_shape=jax.ShapeDtypeStruct((batch * seq, ATTN_WIDTH), BF16),
        compiler_params=_cparams(("arbitrary", "arbitrary", "arbitrary")),
        name="attn_prompt",
    )(sb_bias, q_bf, kb, vb, g_o_row)


def _attn_sample_kernel(pt_ref, q_ref, kn_ref, vn_ref, bias_ref, go_ref, ck_hbm, cv_hbm,
                        o_ref, kbuf, vbuf, sem, *, n_pages, dec_seq, n_seq):
    b = pl.program_id(0)
    rows = N_HEADS * dec_seq
    n_groups = n_pages // PAGE_GROUP

    def group_copies(seq, grp, slot):
        cps = []
        for g in range(PAGE_GROUP):
            page = pt_ref[seq * n_pages + n_pages - 1 - (grp * PAGE_GROUP + g)]
            cps.append(pltpu.make_async_copy(ck_hbm.at[page], kbuf.at[slot, g], sem.at[0, slot, g]))
            cps.append(pltpu.make_async_copy(cv_hbm.at[page], vbuf.at[slot, g], sem.at[1, slot, g]))
        return cps

    def start_group(n):
        @pl.when(n < n_seq * n_groups)
        def _():
            for cp in group_copies(n // n_groups, n % n_groups, n % PAGE_SLOTS):
                cp.start()

    @pl.when(b == 0)
    def _():
        for n in range(PAGE_SLOTS - 1):
            start_group(n)

    heads = range(N_HEADS)
    hs = [slice(h * HEAD_DIM, (h + 1) * HEAD_DIM) for h in heads]
    qr = [slice(h * dec_seq, (h + 1) * dec_seq) for h in heads]
    q = q_ref[0]
    q_h = [q[:, hs[h]].astype(BF16) for h in heads]
    later = _later_matrix(PAGE_SIZE)
    bias = bias_ref[...]

    def blocks(k_of, v_of, n_blocks, carry, mask):
        blks = range(n_blocks)
        zs = [jnp.concatenate(
            [lax.dot_general(q_h[h], k_of(blk, h).astype(BF16), (((1,), (1,)), ((), ())),
                             preferred_element_type=F32) for h in heads], axis=0) + bias for blk in blks]
        ls = [_log_survival(z) for z in zs]
        lms = ls if mask is None else [jnp.where(mask, l, 0.0) for l in ls]
        rs = [_split_dot(lm, later) for lm in lms]
        o_acc, c = carry
        for blk in blks:
            a = jnp.exp(zs[blk] + ls[blk] + rs[blk])
            if mask is not None:
                a = jnp.where(mask, a, 0.0)
            o_blk = jnp.concatenate(
                [jnp.dot(a[qr[h]].astype(BF16), v_of(blk, h).astype(BF16), preferred_element_type=F32)
                 for h in heads], axis=0)
            o_acc = o_acc + jnp.exp(c) * o_blk
            c = c + rs[blk][:, 0:1] + lms[blk][:, 0:1]
        return o_acc, c

    pad = jnp.zeros((PAGE_SIZE - dec_seq, HEAD_DIM), F32)
    k_new, v_new = kn_ref[0], vn_ref[0]
    qi = lax.broadcasted_iota(I32, (rows, PAGE_SIZE), 0) % dec_seq
    kj = lax.broadcasted_iota(I32, (rows, PAGE_SIZE), 1)
    carry = (jnp.zeros((rows, HEAD_DIM), F32), jnp.zeros((rows, 1), F32))
    carry = blocks(lambda blk, h: jnp.concatenate([k_new[:, hs[h]], pad], axis=0),
                   lambda blk, h: jnp.concatenate([v_new[:, hs[h]], pad], axis=0), 1, carry, kj < qi)

    def group_step(t, carry):
        n = b * n_groups + t
        slot = n % PAGE_SLOTS
        for cp in group_copies(0, 0, slot):
            cp.wait()
        start_group(n + PAGE_SLOTS - 1)

        def head_rows(buf):
            return lambda g, h: buf[slot, g, pl.ds(h, PAGE_SIZE, stride=N_HEADS), :]
        return blocks(head_rows(kbuf), head_rows(vbuf), PAGE_GROUP, carry, None)

    o_acc, _ = lax.fori_loop(0, n_groups, group_step, carry)
    ms = jnp.mean(o_acc * o_acc, axis=-1, keepdims=True)
    o_ref[0] = o_acc * lax.rsqrt(ms + EPS) * go_ref[...]


def _attn_sample(page_ids, q_s, k_s, v_s, bias_rows, go_rows, cache_k, cache_v):
    nb, dec_seq, _ = q_s.shape
    n_pages = page_ids.shape[1]
    rows = N_HEADS * dec_seq
    seq_block = pl.BlockSpec((1, dec_seq, ATTN_WIDTH), lambda b, pt: (b, 0, 0))
    grid_spec = pltpu.PrefetchScalarGridSpec(
        num_scalar_prefetch=1,
        grid=(nb,),
        in_specs=[
            seq_block, seq_block, seq_block,
            pl.BlockSpec((rows, PAGE_SIZE), lambda b, pt: (0, 0)),
            pl.BlockSpec((rows, HEAD_DIM), lambda b, pt: (0, 0)),
            pl.BlockSpec(memory_space=pl.ANY),
            pl.BlockSpec(memory_space=pl.ANY),
        ],
        out_specs=pl.BlockSpec((1, rows, HEAD_DIM), lambda b, pt: (b, 0, 0)),
        scratch_shapes=[
            pltpu.VMEM((PAGE_SLOTS, PAGE_GROUP, PAGE_SIZE * N_HEADS, HEAD_DIM), F32),
            pltpu.VMEM((PAGE_SLOTS, PAGE_GROUP, PAGE_SIZE * N_HEADS, HEAD_DIM), F32),
            pltpu.SemaphoreType.DMA((2, PAGE_SLOTS, PAGE_GROUP)),
        ],
    )
    return pl.pallas_call(
        functools.partial(_attn_sample_kernel, n_pages=n_pages, dec_seq=dec_seq, n_seq=nb),
        grid_spec=grid_spec,
        out_shape=jax.ShapeDtypeStruct((nb, rows, HEAD_DIM), F32),
        compiler_params=_cparams(("arbitrary",)),
        name="attn_sample",
    )(page_ids.reshape(-1), q_s, k_s, v_s, bias_rows, go_rows, cache_k, cache_v)


def _conv_kernel(u_ref, hist_ref, w_ref, b_ref, g_ref, lb_ref, beta_ref, y_ref, win_ref, acc_ref, shift_ref,
                 *, tt):
    i = pl.program_id(1)

    @pl.when(i == 0)
    def _():
        win_ref[0:HALO, :] = hist_ref[0]

    @pl.when(i > 0)
    def _():
        win_ref[0:HALO, :] = win_ref[tt:tt + HALO, :]

    win_ref[HALO:HALO + tt, :] = u_ref[0]
    off = HALO - HIST
    def lane_block(cb, _):
        cs = pl.ds(pl.multiple_of(cb * LANES, LANES), LANES)
        acc = jnp.zeros((tt, LANES), F32)
        for s in range(SUBLANES):
            taps = [k for k in range(CONV_KERNEL) if (off + k) % SUBLANES == s]
            span = max(off + k for k in taps) - s + tt
            shift_ref[0:span, :] = win_ref[s:s + span, cs]
            for k in taps:
                base = off + k - s
                acc = acc + w_ref[k:k + 1, cs] * shift_ref[base:base + tt, :]
        acc_ref[:, cs] = acc
        return 0
    lax.fori_loop(0, CONV_CH // LANES, lane_block, 0)
    y = acc_ref[...] + b_ref[...]
    mu = jnp.mean(y, axis=-1, keepdims=True)
    yc = y - mu
    var = jnp.mean(yc * yc, axis=-1, keepdims=True)
    y = yc * lax.rsqrt(var + EPS) * g_ref[...] + lb_ref[...]
    y_ref[0] = (y * jax.nn.sigmoid(y) * beta_ref[...]).astype(BF16)


def _conv(u, hist, w_dw, b_dw, ln_g, ln_b, beta, tt):
    nb, s, c = u.shape
    row = pl.BlockSpec((1, c), lambda b, i: (0, 0))
    return pl.pallas_call(
        functools.partial(_conv_kernel, tt=tt),
        grid=(nb, s // tt),
        in_specs=[
            pl.BlockSpec((1, tt, c), lambda b, i: (b, i, 0)),
            pl.BlockSpec((1, HALO, c), lambda b, i: (b, 0, 0)),
            pl.BlockSpec((CONV_KERNEL, c), lambda b, i: (0, 0)),
            row, row, row, row,
        ],
        out_specs=pl.BlockSpec((1, tt, c), lambda b, i: (b, i, 0)),
        out_shape=jax.ShapeDtypeStruct((nb, s, c), BF16),
        scratch_shapes=[pltpu.VMEM((HALO + tt, c), F32), pltpu.VMEM((tt, c), F32),
                        pltpu.VMEM((HALO + tt, LANES), F32)],
        compiler_params=_cparams(("arbitrary", "arbitrary")),
        name="conv",
    )(u, hist, w_dw, b_dw, ln_g, ln_b, beta)


R_E1, R_E2, R_W1, R_W2, R_RANK1, R_RANK2 = range(6)
GRP_LANE0 = N_EXPERTS


def _lane_pick(x, lane, idx):
    return jnp.sum(jnp.where(lane == idx, x, 0.0), axis=-1, keepdims=True)


def _post_kernel(hp_ref, hs_ref, op_ref, os_ref, ycp_ref, ycs_ref, wt_ref, wb_ref, g_ref, wr_ref, br_ref,
                 h1_ref, xn_ref, route_ref, tot_ref, run_ref, *, n_prompt_tiles):
    i = pl.program_id(0)

    @pl.when(i == 0)
    def _():
        run_ref[...] = jnp.zeros_like(run_ref)

    is_prompt = i < n_prompt_tiles
    pick = lambda p_ref, s_ref: jnp.where(is_prompt, p_ref[...], s_ref[...])
    h1 = (pick(hp_ref, hs_ref)
          + jnp.dot(pick(op_ref, os_ref), wt_ref[...], preferred_element_type=F32)
          + jnp.dot(pick(ycp_ref, ycs_ref), wb_ref[...], preferred_element_type=F32))
    h1_ref[...] = h1
    ms = jnp.mean(h1 * h1, axis=-1, keepdims=True)
    xn = h1 * lax.rsqrt(ms + EPS) * g_ref[...]
    for cb in range(ROW_SLAB):
        xn_ref[pl.ds(cb, xn.shape[0], stride=ROW_SLAB), :] = xn[:, cb * LANES:(cb + 1) * LANES]

    x_hi = xn.astype(BF16)
    x_lo = (xn - x_hi.astype(F32)).astype(BF16)
    logits = (jnp.dot(x_hi, wr_ref[0], preferred_element_type=F32)
              + jnp.dot(x_lo, wr_ref[0], preferred_element_type=F32)
              + jnp.dot(x_hi, wr_ref[1], preferred_element_type=F32)) + br_ref[...]
    tm = logits.shape[0]
    lane = lax.broadcasted_iota(I32, (tm, LANES), 1)
    neg = -jnp.inf
    is_grp = (lane >= GRP_LANE0) & (lane < GRP_LANE0 + N_GROUPS)
    glog = jnp.where(is_grp, logits, neg)
    gmax = jnp.max(glog, axis=-1, keepdims=True)
    gidx = jnp.min(jnp.where(glog == gmax, lane - GRP_LANE0, LANES), axis=-1, keepdims=True)
    g_w = 1.0 / jnp.sum(jnp.exp(glog - gmax), axis=-1, keepdims=True)

    in_grp = (lane < N_EXPERTS) & (lane // EXPERTS_PER_GROUP == gidx)
    elog = jnp.where(in_grp, logits, neg)
    v1 = jnp.max(elog, axis=-1, keepdims=True)
    i1 = jnp.min(jnp.where(elog == v1, lane, LANES), axis=-1, keepdims=True)
    elog2 = jnp.where(lane == i1, neg, elog)
    v2 = jnp.max(elog2, axis=-1, keepdims=True)
    i2 = jnp.min(jnp.where(elog2 == v2, lane, LANES), axis=-1, keepdims=True)
    ew = jnp.exp(v2 - v1)
    w1 = g_w / (1.0 + ew)
    w2 = g_w * ew / (1.0 + ew)

    onehot = ((lane == i1) | (lane == i2)).astype(F32)
    trow = lax.broadcasted_iota(I32, (tm, tm), 0)
    tcol = lax.broadcasted_iota(I32, (tm, tm), 1)
    earlier = (tcol < trow).astype(BF16)
    before = run_ref[0:1, :] + jnp.dot(earlier, onehot.astype(BF16), preferred_element_type=F32)
    rank1 = _lane_pick(before, lane, i1)
    rank2 = _lane_pick(before, lane, i2)
    run_ref[...] = run_ref[...] + jnp.sum(onehot, axis=0, keepdims=True)
    tot_ref[...] = run_ref[...]

    rec = jnp.zeros((tm, LANES), F32)
    for idx, val in ((R_E1, i1.astype(F32)), (R_E2, i2.astype(F32)), (R_W1, w1), (R_W2, w2),
                     (R_RANK1, rank1), (R_RANK2, rank2)):
        rec = jnp.where(lane == idx, val, rec)
    route_ref[...] = rec


def _post(h_p, h_s, o_p, o_s, yc_p, yc_s, w_top, w_bot, g_ffn, w_router, b_router):
    npt = h_p.shape[0] // TOK_TM
    nst = h_s.shape[0] // TOK_TM
    t = (npt + nst) * TOK_TM
    tok = lambda w: pl.BlockSpec((TOK_TM, w), lambda i: (i, 0))
    full = lambda a: pl.BlockSpec(a.shape, lambda i: (0,) * a.ndim)
    prompt_tile = lambda w: pl.BlockSpec((TOK_TM, w), lambda i: (jnp.minimum(i, npt - 1), 0))
    sample_tile = lambda w: pl.BlockSpec((TOK_TM, w), lambda i: (jnp.clip(i - npt, 0, nst - 1), 0))
    return pl.pallas_call(
        functools.partial(_post_kernel, n_prompt_tiles=npt),
        grid=(npt + nst,),
        in_specs=[prompt_tile(D_MODEL), sample_tile(D_MODEL), prompt_tile(ATTN_WIDTH), sample_tile(ATTN_WIDTH),
                  prompt_tile(CONV_CH), sample_tile(CONV_CH), full(w_top), full(w_bot),
                  full(g_ffn), full(w_router), full(b_router)],
        out_specs=[tok(D_MODEL), pl.BlockSpec((TOK_TM * ROW_SLAB, LANES), lambda i: (i, 0)), tok(LANES),
                   pl.BlockSpec((8, LANES), lambda i: (0, 0))],
        out_shape=[jax.ShapeDtypeStruct((t, D_MODEL), F32), jax.ShapeDtypeStruct((t * ROW_SLAB, LANES), F32),
                   jax.ShapeDtypeStruct((t, LANES), F32), jax.ShapeDtypeStruct((8, LANES), F32)],
        scratch_shapes=[pltpu.VMEM((8, LANES), F32)],
        compiler_params=_cparams(("arbitrary",)),
        name="post",
    )(h_p, h_s, o_p, o_s, yc_p, yc_s, w_top, w_bot, g_ffn, w_router, b_router)


M_TILE_EXPERT, M_NUM_TILES = 0, 1


def _positions_kernel(route_ref, tot_ref, pos_ref, meta_ref):
    t = route_ref.shape[0]
    lane1 = lax.broadcasted_iota(I32, (8, LANES), 1)
    total = tot_ref[...]
    ntile = jnp.where(lane1 < N_EXPERTS, jnp.ceil(total / ROW_TM), 0.0)
    srow = lax.broadcasted_iota(I32, (LANES, LANES), 0)
    scol = lax.broadcasted_iota(I32, (LANES, LANES), 1)
    start_tile = jnp.dot(ntile.astype(BF16), (srow < scol).astype(BF16), preferred_element_type=F32)
    start_row = start_tile[0:1, :] * ROW_TM

    route = route_ref[...]
    lane = lax.broadcasted_iota(I32, (t, LANES), 1)
    e1 = _lane_pick(route, lane, R_E1).astype(I32)
    e2 = _lane_pick(route, lane, R_E2).astype(I32)
    p1 = _lane_pick(jnp.broadcast_to(start_row, (t, LANES)), lane, e1) + _lane_pick(route, lane, R_RANK1)
    p2 = _lane_pick(jnp.broadcast_to(start_row, (t, LANES)), lane, e2) + _lane_pick(route, lane, R_RANK2)
    pos_ref[...] = jnp.where(lane == 0, p1, jnp.where(lane == 1, p2, 0.0)).astype(I32)

    end_tile = start_tile[0:1, :] + ntile[0:1, :]
    end_col = jnp.sum(jnp.where(srow == scol, jnp.broadcast_to(end_tile, (LANES, LANES)), 0.0),
                      axis=-1, keepdims=True)
    done = ((end_col <= scol.astype(F32)) & (srow < N_EXPERTS)).astype(F32)
    tile_expert = jnp.minimum(jnp.sum(done, axis=0, keepdims=True), N_EXPERTS - 1.0)
    num_tiles = jnp.sum(ntile[0:1, :], axis=-1, keepdims=True)
    row8 = lax.broadcasted_iota(I32, (8, LANES), 0)
    meta = jnp.where(row8 == M_TILE_EXPERT, jnp.broadcast_to(tile_expert, (8, LANES)),
                     jnp.where(row8 == M_NUM_TILES, jnp.broadcast_to(num_tiles, (8, LANES)), 0.0))
    meta_ref[...] = meta.astype(I32)


def _positions(route, totals):
    t = route.shape[0]
    return pl.pallas_call(
        _positions_kernel,
        out_shape=[jax.ShapeDtypeStruct((t, LANES), I32), jax.ShapeDtypeStruct((8, LANES), I32)],
        compiler_params=pltpu.CompilerParams(vmem_limit_bytes=VMEM_LIMIT),
        name="positions",
    )(route, totals)


def _experts_kernel(te_ref, nt_ref, p1_ref, p2_ref, xn_hbm, wg_ref, wu_ref, wd_ref,
                    y_ref, row_tok, xg, wg_bf, wu_bf, wd_bf, sem, *, n_tok, n_rows):
    j = pl.program_id(0)
    n_tiles = nt_ref[0]

    def row_copy(r, slot, tok):
        src = xn_hbm.at[pl.ds(pl.multiple_of(tok * ROW_SLAB, ROW_SLAB), ROW_SLAB), :]
        dst = xg.at[slot, pl.ds(pl.multiple_of(r * ROW_SLAB, ROW_SLAB), ROW_SLAB), :]
        return pltpu.make_async_copy(src, dst, sem.at[slot])

    def gather(tile, slot):
        def issue(r, _):
            row_copy(r, slot, row_tok[tile * ROW_TM + r]).start()
            return 0
        lax.fori_loop(0, ROW_TM, issue, 0, unroll=DMA_UNROLL)

    @pl.when(j == 0)
    def _():
        def clear(r, _):
            row_tok[r] = 0
            return 0
        lax.fori_loop(0, n_rows, clear, 0, unroll=DMA_UNROLL)

        def place(t, _):
            row_tok[p1_ref[t]] = t
            row_tok[p2_ref[t]] = t
            return 0
        lax.fori_loop(0, n_tok, place, 0, unroll=DMA_UNROLL)
        gather(0, 0)

    @pl.when(j < n_tiles)
    def _():
        slot = j % 2

        @pl.when(j + 1 < n_tiles)
        def _():
            gather(j + 1, 1 - slot)

        new_expert = (j == 0) | (te_ref[j] != te_ref[jnp.maximum(j - 1, 0)])

        @pl.when(new_expert)
        def _():
            wg_bf[...] = wg_ref[0].astype(BF16)
            wu_bf[...] = wu_ref[0].astype(BF16)
            wd_bf[...] = wd_ref[0].astype(BF16)

        def drain(r, _):
            row_copy(r, slot, 0).wait()
            return 0
        lax.fori_loop(0, ROW_TM, drain, 0, unroll=DMA_UNROLL)

        x = jnp.concatenate([xg[slot, pl.ds(cb, ROW_TM, stride=ROW_SLAB), :].astype(BF16)
                             for cb in range(ROW_SLAB)], axis=1)
        hg = jnp.dot(x, wg_bf[...], preferred_element_type=F32)
        hu = jnp.dot(x, wu_bf[...], preferred_element_type=F32)
        hid = (hg * jax.nn.sigmoid(hg) * hu).astype(BF16)
        y_ref[...] = jnp.dot(hid, wd_bf[...], preferred_element_type=F32)

    @pl.when(j >= nt_ref[0])
    def _():
        y_ref[...] = jnp.zeros_like(y_ref)


def _experts(tile_expert, num_tiles, pos1, pos2, xn, w_gate, w_up, w_down):
    n_tok = xn.shape[0] // ROW_SLAB
    max_tiles = (2 * n_tok + N_EXPERTS * (ROW_TM - 1)) // ROW_TM
    n_rows = max_tiles * ROW_TM
    w_in_spec = pl.BlockSpec((1, D_MODEL, D_EXPERT), lambda j, te, nt, p1, p2: (te[j], 0, 0))
    grid_spec = pltpu.PrefetchScalarGridSpec(
        num_scalar_prefetch=4,
        grid=(max_tiles,),
        in_specs=[
            pl.BlockSpec(memory_space=pl.ANY),
            w_in_spec, w_in_spec,
            pl.BlockSpec((1, D_EXPERT, D_MODEL), lambda j, te, nt, p1, p2: (te[j], 0, 0)),
        ],
        out_specs=pl.BlockSpec((ROW_TM, D_MODEL), lambda j, te, nt, p1, p2: (j, 0)),
        scratch_shapes=[
            pltpu.SMEM((n_rows,), I32),
            pltpu.VMEM((2, ROW_TM * ROW_SLAB, LANES), F32),
            pltpu.VMEM((D_MODEL, D_EXPERT), BF16),
            pltpu.VMEM((D_MODEL, D_EXPERT), BF16),
            pltpu.VMEM((D_EXPERT, D_MODEL), BF16),
            pltpu.SemaphoreType.DMA((2,)),
        ],
    )
    return pl.pallas_call(
        functools.partial(_experts_kernel, n_tok=n_tok, n_rows=n_rows),
        grid_spec=grid_spec,
        out_shape=jax.ShapeDtypeStruct((n_rows, D_MODEL), F32),
        compiler_params=_cparams(("arbitrary",)),
        name="experts",
    )(tile_expert, num_tiles, pos1, pos2, xn, w_gate, w_up, w_down)


def _final_kernel(p1_ref, p2_ref, h1_ref, route_ref, pp_ref, ps_ref, ys_hbm, g_ref, wpg_ref, wpp_ref,
                  outp_ref, outs_ref, y1, y2, sem, *, n_prompt_tiles):
    i = pl.program_id(0)
    slot = i % 2

    def row_copies(r, slot, pos1, pos2):
        return (pltpu.make_async_copy(ys_hbm.at[pl.ds(pos1, 1), :], y1.at[slot, pl.ds(r, 1), :], sem.at[0, slot]),
                pltpu.make_async_copy(ys_hbm.at[pl.ds(pos2, 1), :], y2.at[slot, pl.ds(r, 1), :], sem.at[1, slot]))

    def gather(tile, slot):
        def issue(r, _):
            tok = tile * TOK_TM + r
            for cp in row_copies(r, slot, p1_ref[tok], p2_ref[tok]):
                cp.start()
            return 0
        lax.fori_loop(0, TOK_TM, issue, 0, unroll=DMA_UNROLL)

    def drain(slot):
        def wait_row(r, _):
            for cp in row_copies(r, slot, 0, 0):
                cp.wait()
            return 0
        lax.fori_loop(0, TOK_TM, wait_row, 0, unroll=DMA_UNROLL)

    @pl.when(i == 0)
    def _():
        gather(0, 0)

    @pl.when(i + 1 < pl.num_programs(0))
    def _():
        gather(i + 1, 1 - slot)

    route = route_ref[...]
    lane = lax.broadcasted_iota(I32, route.shape, 1)
    w1 = _lane_pick(route, lane, R_W1)
    w2 = _lane_pick(route, lane, R_W2)
    p = jnp.where(i < n_prompt_tiles, pp_ref[...], ps_ref[...])
    emb = jnp.dot(p.astype(BF16), wpp_ref[...], preferred_element_type=F32)

    drain(slot)
    h2 = h1_ref[...] + w1 * y1[slot] + w2 * y2[slot]
    ms = jnp.mean(h2 * h2, axis=-1, keepdims=True)
    hn = (h2 * lax.rsqrt(ms + EPS) * g_ref[...]).astype(BF16)
    gate = jax.nn.sigmoid(jnp.dot(hn, wpg_ref[...], preferred_element_type=F32))
    res = h2 + gate * emb

    @pl.when(i < n_prompt_tiles)
    def _():
        outp_ref[...] = res

    @pl.when(i >= n_prompt_tiles)
    def _():
        outs_ref[...] = res


def _final(pos1, pos2, h1, route, p_p, p_s, ys, g_ple, w_ple_gate_bf, w_ple_proj_bf):
    t = h1.shape[0]
    n_prompt = p_p.shape[0]
    npt = n_prompt // TOK_TM
    nst = (t - n_prompt) // TOK_TM
    tok = lambda w: pl.BlockSpec((TOK_TM, w), lambda i, p1, p2: (i, 0))
    full = lambda a: pl.BlockSpec(a.shape, lambda i, p1, p2: (0,) * a.ndim)
    prompt_tile = lambda w: pl.BlockSpec((TOK_TM, w), lambda i, p1, p2: (jnp.minimum(i, npt - 1), 0))
    sample_tile = lambda w: pl.BlockSpec((TOK_TM, w), lambda i, p1, p2: (jnp.clip(i - npt, 0, nst - 1), 0))
    grid_spec = pltpu.PrefetchScalarGridSpec(
        num_scalar_prefetch=2,
        grid=(t // TOK_TM,),
        in_specs=[tok(D_MODEL), tok(LANES), prompt_tile(PLE_DIM), sample_tile(PLE_DIM),
                  pl.BlockSpec(memory_space=pl.ANY),
                  full(g_ple), full(w_ple_gate_bf), full(w_ple_proj_bf)],
        out_specs=[prompt_tile(D_MODEL), sample_tile(D_MODEL)],
        scratch_shapes=[pltpu.VMEM((2, TOK_TM, D_MODEL), F32), pltpu.VMEM((2, TOK_TM, D_MODEL), F32),
                        pltpu.SemaphoreType.DMA((2, 2))],
    )
    return pl.pallas_call(
        functools.partial(_final_kernel, n_prompt_tiles=npt),
        grid_spec=grid_spec,
        out_shape=[jax.ShapeDtypeStruct((n_prompt, D_MODEL), F32),
                   jax.ShapeDtypeStruct((t - n_prompt, D_MODEL), F32)],
        compiler_params=_cparams(("arbitrary",)),
        name="final",
    )(pos1, pos2, h1, route, p_p, p_s, ys, g_ple, w_ple_gate_bf, w_ple_proj_bf)


def _layer(i, hp, hs, cache_k, cache_v, state_conv, page_table, p_prompt, p_sample,
           g_mix_norm, w_in, g_q, g_k, sb_bias, g_o, w_dw, b_dw, ln_g, ln_b, beta_conv, w_out,
           g_ffn_norm, w_grp, b_grp, w_rt, b_rt, w_gate, w_up, w_down,
           g_ple_norm, w_ple_proj, w_ple_gate):
    batch, seq, _ = hp.shape
    nb, dec_seq, _ = hs.shape
    n_prompt = batch * seq
    n_sample = nb * dec_seq
    row = lambda a: a.reshape(1, -1)

    x_p = hp.reshape(n_prompt, D_MODEL)
    x_s = hs.reshape(n_sample, D_MODEL)
    mix_args = (row(g_mix_norm[i]), w_in[i].astype(BF16), row(g_q[i]), row(g_k[i]))
    q_p, k_p, v_p, kb_p, vb_p, u_p = _mix_in(x_p, *mix_args)
    q_s, k_s, v_s, _, _, u_s = _mix_in(x_s, *mix_args)

    o_p = _attn_prompt(sb_bias[i], q_p, kb_p, vb_p, row(g_o[i]), batch, seq)

    rows = N_HEADS * dec_seq
    split = lambda a: a.astype(F32).reshape(nb, dec_seq, ATTN_WIDTH)
    bias_rows = jnp.broadcast_to(jnp.repeat(sb_bias[i], dec_seq)[:, None], (rows, PAGE_SIZE))
    go_rows = jnp.repeat(g_o[i], dec_seq, axis=0)
    depth, n_pool = cache_k.shape[:2]
    o_s = _attn_sample(page_table + i * n_pool, split(q_s), split(k_s), split(v_s), bias_rows, go_rows,
                       cache_k.reshape(depth * n_pool, PAGE_SIZE * N_HEADS, HEAD_DIM),
                       cache_v.reshape(depth * n_pool, PAGE_SIZE * N_HEADS, HEAD_DIM))
    o_s = o_s.reshape(nb, N_HEADS, dec_seq, HEAD_DIM).transpose(0, 2, 1, 3).reshape(n_sample, ATTN_WIDTH)

    conv_args = (w_dw[i], row(b_dw[i]), row(ln_g[i]), row(ln_b[i]), row(beta_conv[i]))
    u_p = u_p.reshape(batch, seq, CONV_CH)
    u_s = u_s.reshape(nb, dec_seq, CONV_CH)
    hist_p = jnp.zeros((batch, HALO, CONV_CH), F32)
    hist_s = jnp.pad(state_conv[i], ((0, 0), (HALO - HIST, 0), (0, 0)))
    yc_p = _conv(u_p, hist_p, *conv_args, tt=CONV_TT)
    yc_s = _conv(u_s, hist_s, *conv_args, tt=dec_seq)

    w_out_bf = w_out[i].astype(BF16)
    w_router = jnp.zeros((D_MODEL, LANES), F32)
    w_router = w_router.at[:, :N_EXPERTS].set(w_rt[i]).at[:, GRP_LANE0:GRP_LANE0 + N_GROUPS].set(w_grp[i])
    b_router = jnp.zeros((1, LANES), F32)
    b_router = b_router.at[0, :N_EXPERTS].set(b_rt[i]).at[0, GRP_LANE0:GRP_LANE0 + N_GROUPS].set(b_grp[i])
    w_router_hi = w_router.astype(BF16)
    w_router_lo = (w_router - w_router_hi.astype(F32)).astype(BF16)
    h1, xn, route, totals = _post(x_p, x_s, o_p, o_s.astype(BF16), yc_p.reshape(n_prompt, CONV_CH),
                                  yc_s.reshape(n_sample, CONV_CH), w_out_bf[:ATTN_WIDTH], w_out_bf[ATTN_WIDTH:],
                                  row(g_ffn_norm[i]), jnp.stack([w_router_hi, w_router_lo]), b_router)

    pos, meta = _positions(route, totals)
    pos1, pos2 = pos[:, 0], pos[:, 1]
    ys = _experts(meta[M_TILE_EXPERT], meta[M_NUM_TILES, :1], pos1, pos2, xn, w_gate[i], w_up[i], w_down[i])

    y_p, y_s = _final(pos1, pos2, h1, route, p_prompt[i].reshape(n_prompt, PLE_DIM),
                      p_sample[i].reshape(n_sample, PLE_DIM), ys, row(g_ple_norm[i]),
                      w_ple_gate[i].astype(BF16), w_ple_proj[i].astype(BF16))

    heads = lambda a, b_, s_: a.reshape(b_, s_, N_HEADS, HEAD_DIM)
    conv_p = u_p[:, seq - HIST:]
    conv_s = jnp.concatenate([state_conv[i], u_s], axis=1)[:, -HIST:]
    return (y_p.reshape(batch, seq, D_MODEL), y_s.reshape(nb, dec_seq, D_MODEL),
            heads(k_p, batch, seq), heads(v_p, batch, seq), conv_p,
            heads(k_s, nb, dec_seq), heads(v_s, nb, dec_seq), conv_s)


def kernel(x_prompt, x_sample, cache_k, cache_v, state_conv, page_table, p_prompt, p_sample, g_mix_norm, w_in, g_q, g_k, sb_bias, g_o, w_dw, b_dw, ln_g, ln_b, beta_conv, w_out, g_ffn_norm, w_grp, b_grp, w_rt, b_rt, w_gate, w_up, w_down, g_ple_norm, w_ple_proj, w_ple_gate):
    depth = w_in.shape[0]
    hp, hs = x_prompt, x_sample
    outs = []
    for i in range(depth):
        res = _layer(i, hp, hs, cache_k, cache_v, state_conv, page_table, p_prompt, p_sample,
                     g_mix_norm, w_in, g_q, g_k, sb_bias, g_o, w_dw, b_dw, ln_g, ln_b, beta_conv, w_out,
                     g_ffn_norm, w_grp, b_grp, w_rt, b_rt, w_gate, w_up, w_down,
                     g_ple_norm, w_ple_proj, w_ple_gate)
        hp, hs = res[0], res[1]
        outs.append(res[2:])
    stacked = tuple(jnp.stack([o[n] for o in outs], axis=0) for n in range(6))
    return (hp, hs) + stacked
```

```python
import functools

import jax
import jax.numpy as jnp
from jax import lax
from jax.experimental import pallas as pl
from jax.experimental.pallas import tpu as pltpu

F32 = jnp.float32
BF16 = jnp.bfloat16
I32 = jnp.int32

D_MODEL = 2048
HEAD_DIM = 128
N_HEADS = 8
ATTN_WIDTH = N_HEADS * HEAD_DIM
CONV_CH = D_MODEL - ATTN_WIDTH
CONV_KERNEL = 31
HIST = CONV_KERNEL - 1
N_GROUPS = 4
EXPERTS_PER_GROUP = 8
N_EXPERTS = N_GROUPS * EXPERTS_PER_GROUP
D_EXPERT = D_MODEL // 4
PLE_DIM = 256
PAGE_SIZE = 128
EPS = 1e-6

LANES = 128
SUBLANES = 8
ROW_SLAB = D_MODEL // LANES
VMEM_LIMIT = 56 * 1024 * 1024

MIX_TM = 1024
MIX_TN = 512
ATT_T = 256
ATT_HEADS = 8
PAGE_GROUP = 4
PAGE_SLOTS = 3
CONV_TT = 256
HALO = 32
TOK_TM = 256
ROW_TM = 256
DMA_UNROLL = 8


def _cparams(sem):
    return pltpu.CompilerParams(dimension_semantics=sem, vmem_limit_bytes=VMEM_LIMIT)


def _log_survival(z):
    nz = -z
    return jnp.minimum(nz, 0.0) - jnp.log(1.0 + jnp.exp(jnp.minimum(z, nz)))


def _later_matrix(n):
    row = lax.broadcasted_iota(I32, (n, n), 0)
    col = lax.broadcasted_iota(I32, (n, n), 1)
    return (row > col).astype(BF16)


def _split_dot(l, m):
    hi = l.astype(BF16)
    lo = (l - hi.astype(F32)).astype(BF16)
    return (jnp.dot(hi, m, preferred_element_type=F32)
            + jnp.dot(lo, m, preferred_element_type=F32))


def _mix_kernel(x_ref, g_ref, wa_ref, wb_ref, gq_ref, gk_ref,
                q_ref, k_ref, v_ref, kb_ref, vb_ref, u_ref, xn_ref):
    j = pl.program_id(1)

    @pl.when(j == 0)
    def _():
        x = x_ref[...]
        ms = jnp.mean(x * x, axis=-1, keepdims=True)
        xn_ref[...] = (x * lax.rsqrt(ms + EPS) * g_ref[...]).astype(BF16)

    a = xn_ref[...]
    z = jnp.dot(a, wa_ref[...], preferred_element_type=F32)

    def head_norm(g):
        outs = []
        for hh in range(MIX_TN // HEAD_DIM):
            zh = z[:, hh * HEAD_DIM:(hh + 1) * HEAD_DIM]
            ms = jnp.mean(zh * zh, axis=-1, keepdims=True)
            outs.append(zh * lax.rsqrt(ms + EPS) * g)
        return jnp.concatenate(outs, axis=-1)

    @pl.when(j < 2)
    def _():
        q_ref[...] = (head_norm(gq_ref[...]) * (HEAD_DIM ** -0.5)).astype(BF16)

    @pl.when((j >= 2) & (j < 4))
    def _():
        kn = head_norm(gk_ref[...])
        k_ref[...] = kn
        kb_ref[...] = kn.astype(BF16)

    @pl.when((j >= 4) & (j < 6))
    def _():
        v_ref[...] = z
        vb_ref[...] = z.astype(BF16)

    @pl.when(j >= 6)
    def _():
        zb = jnp.dot(a, wb_ref[...], preferred_element_type=F32)
        u_ref[...] = z * jax.nn.sigmoid(zb)


def _mix_in(x, g_norm, w_in_bf, g_q, g_k):
    t = x.shape[0]
    tm = min(MIX_TM, t)
    nt = t // tm
    half = ATTN_WIDTH // MIX_TN

    def part_map(p):
        return lambda i, j: (i, jnp.clip(j - p * half, 0, half - 1))

    f32_out = jax.ShapeDtypeStruct((t, ATTN_WIDTH), F32)
    bf_out = jax.ShapeDtypeStruct((t, ATTN_WIDTH), BF16)
    out_block = lambda p: pl.BlockSpec((tm, MIX_TN), part_map(p))
    return pl.pallas_call(
        _mix_kernel,
        grid=(nt, 4 * half),
        in_specs=[
            pl.BlockSpec((tm, D_MODEL), lambda i, j: (i, 0)),
            pl.BlockSpec((1, D_MODEL), lambda i, j: (0, 0)),
            pl.BlockSpec((D_MODEL, MIX_TN), lambda i, j: (0, j)),
            pl.BlockSpec((D_MODEL, MIX_TN), lambda i, j: (0, jnp.maximum(j, 3 * half) + half)),
            pl.BlockSpec((1, HEAD_DIM), lambda i, j: (0, 0)),
            pl.BlockSpec((1, HEAD_DIM), lambda i, j: (0, 0)),
        ],
        out_specs=[out_block(0), out_block(1), out_block(2), out_block(1), out_block(2), out_block(3)],
        out_shape=[bf_out, f32_out, f32_out, bf_out, bf_out, f32_out],
        scratch_shapes=[pltpu.VMEM((tm, D_MODEL), BF16)],
        compiler_params=_cparams(("arbitrary", "arbitrary")),
        name="mix_in",
    )(x, g_norm, w_in_bf, w_in_bf, g_q, g_k)


def _attn_prompt_kernel(bias_ref, q_ref, k_ref, v_ref, go_ref, o_ref):
    hg = pl.program_id(1)
    i = pl.program_id(2)
    later = _later_matrix(ATT_T)
    heads = range(ATT_HEADS)
    hs = [slice(n * HEAD_DIM, (n + 1) * HEAD_DIM) for n in heads]
    bias = [bias_ref[hg * ATT_HEADS + n] for n in heads]
    qs = [q_ref[:, hs[n]] for n in heads]
    causal = (lax.broadcasted_iota(I32, (ATT_T, ATT_T), 1) < lax.broadcasted_iota(I32, (ATT_T, ATT_T), 0))

    def block(kb, carry, masked):
        ks = pl.multiple_of(kb * ATT_T, ATT_T)
        zs = [lax.dot_general(qs[n], k_ref[pl.ds(ks, ATT_T), hs[n]], (((1,), (1,)), ((), ())),
                              preferred_element_type=F32) + bias[n] for n in heads]
        ls = [_log_survival(z) for z in zs]
        lms = [jnp.where(causal, l, 0.0) for l in ls] if masked else ls
        rs = [_split_dot(lm, later) for lm in lms]
        avs = []
        for n in heads:
            a = jnp.exp(zs[n] + ls[n] + rs[n])
            if masked:
                a = jnp.where(causal, a, 0.0)
            avs.append(jnp.dot(a.astype(BF16), v_ref[pl.ds(ks, ATT_T), hs[n]], preferred_element_type=F32))
        return tuple((carry[n][0] + jnp.exp(carry[n][1]) * avs[n],
                      carry[n][1] + rs[n][:, 0:1] + lms[n][:, 0:1]) for n in heads)

    carry = tuple((jnp.zeros((ATT_T, HEAD_DIM), F32), jnp.zeros((ATT_T, 1), F32)) for _ in heads)
    carry = block(i, carry, True)
    carry = lax.fori_loop(0, i, lambda t, cr: block(i - 1 - t, cr, False), carry)
    for n in heads:
        o_acc = carry[n][0]
        ms = jnp.mean(o_acc * o_acc, axis=-1, keepdims=True)
        o_ref[:, hs[n]] = (o_acc * lax.rsqrt(ms + EPS) * go_ref[:, hs[n]]).astype(BF16)


def _attn_prompt(sb_bias, q_bf, kb, vb, g_o_row, batch, seq):
    nq = seq // ATT_T
    width = ATT_HEADS * HEAD_DIM
    return pl.pallas_call(
        _attn_prompt_kernel,
        grid=(batch, N_HEADS // ATT_HEADS, nq),
        in_specs=[
            pl.BlockSpec(memory_space=pltpu.SMEM),
            pl.BlockSpec((ATT_T, width), lambda b, h, i: (b * nq + i, h)),
            pl.BlockSpec((seq, width), lambda b, h, i: (b, h)),
            pl.BlockSpec((seq, width), lambda b, h, i: (b, h)),
            pl.BlockSpec((1, width), lambda b, h, i: (0, h)),
        ],
        out_specs=pl.BlockSpec((ATT_T, width), lambda b, h, i: (b * nq + i, h)),
        out_shape=jax.ShapeDtypeStruct((batch * seq, ATTN_WIDTH), BF16),
        compiler_params=_cparams(("arbitrary", "arbitrary", "arbitrary")),
        name="attn_prompt",
    )(sb_bias, q_bf, kb, vb, g_o_row)


def _attn_sample_kernel(pt_ref, q_ref, kn_ref, vn_ref, bias_ref, go_ref, ck_hbm, cv_hbm,
                        o_ref, kbuf, vbuf, sem, *, n_pages, dec_seq, n_seq):
    b = pl.program_id(0)
    rows = N_HEADS * dec_seq
    n_groups = n_pages // PAGE_GROUP

    def group_copies(seq, grp, slot):
        cps = []
        for g in range(PAGE_GROUP):
            page = pt_ref[seq * n_pages + n_pages - 1 - (grp * PAGE_GROUP + g)]
            cps.append(pltpu.make_async_copy(ck_hbm.at[page], kbuf.at[slot, g], sem.at[0, slot, g]))
            cps.append(pltpu.make_async_copy(cv_hbm.at[page], vbuf.at[slot, g], sem.at[1, slot, g]))
        return cps

    def start_group(n):
        @pl.when(n < n_seq * n_groups)
        def _():
            for cp in group_copies(n // n_groups, n % n_groups, n % PAGE_SLOTS):
                cp.start()

    @pl.when(b == 0)
    def _():
        for n in range(PAGE_SLOTS - 1):
            start_group(n)

    heads = range(N_HEADS)
    hs = [slice(h * HEAD_DIM, (h + 1) * HEAD_DIM) for h in heads]
    qr = [slice(h * dec_seq, (h + 1) * dec_seq) for h in heads]
    q = q_ref[0]
    q_h = [q[:, hs[h]].astype(BF16) for h in heads]
    later = _later_matrix(PAGE_SIZE)
    bias = bias_ref[...]

    def blocks(k_of, v_of, n_blocks, carry, mask):
        blks = range(n_blocks)
        zs = [jnp.concatenate(
            [lax.dot_general(q_h[h], k_of(blk, h).astype(BF16), (((1,), (1,)), ((), ())),
                             preferred_element_type=F32) for h in heads], axis=0) + bias for blk in blks]
        ls = [_log_survival(z) for z in zs]
        lms = ls if mask is None else [jnp.where(mask, l, 0.0) for l in ls]
        rs = [_split_dot(lm, later) for lm in lms]
        o_acc, c = carry
        for blk in blks:
            a = jnp.exp(zs[blk] + ls[blk] + rs[blk])
            if mask is not None:
                a = jnp.where(mask, a, 0.0)
            o_blk = jnp.concatenate(
                [jnp.dot(a[qr[h]].astype(BF16), v_of(blk, h).astype(BF16), preferred_element_type=F32)
                 for h in heads], axis=0)
            o_acc = o_acc + jnp.exp(c) * o_blk
            c = c + rs[blk][:, 0:1] + lms[blk][:, 0:1]
        return o_acc, c

    pad = jnp.zeros((PAGE_SIZE - dec_seq, HEAD_DIM), F32)
    k_new, v_new = kn_ref[0], vn_ref[0]
    qi = lax.broadcasted_iota(I32, (rows, PAGE_SIZE), 0) % dec_seq
    kj = lax.broadcasted_iota(I32, (rows, PAGE_SIZE), 1)
    carry = (jnp.zeros((rows, HEAD_DIM), F32), jnp.zeros((rows, 1), F32))
    carry = blocks(lambda blk, h: jnp.concatenate([k_new[:, hs[h]], pad], axis=0),
                   lambda blk, h: jnp.concatenate([v_new[:, hs[h]], pad], axis=0), 1, carry, kj < qi)

    def group_step(t, carry):
        n = b * n_groups + t
        slot = n % PAGE_SLOTS
        for cp in group_copies(0, 0, slot):
            cp.wait()
        start_group(n + PAGE_SLOTS - 1)

        def head_rows(buf):
            return lambda g, h: buf[slot, g, pl.ds(h, PAGE_SIZE, stride=N_HEADS), :]
        return blocks(head_rows(kbuf), head_rows(vbuf), PAGE_GROUP, carry, None)

    o_acc, _ = lax.fori_loop(0, n_groups, group_step, carry)
    ms = jnp.mean(o_acc * o_acc, axis=-1, keepdims=True)
    o_ref[0] = o_acc * lax.rsqrt(ms + EPS) * go_ref[...]


def _attn_sample(page_ids, q_s, k_s, v_s, bias_rows, go_rows, cache_k, cache_v):
    nb, dec_seq, _ = q_s.shape
    n_pages = page_ids.shape[1]
    rows = N_HEADS * dec_seq
    seq_block = pl.BlockSpec((1, dec_seq, ATTN_WIDTH), lambda b, pt: (b, 0, 0))
    grid_spec = pltpu.PrefetchScalarGridSpec(
        num_scalar_prefetch=1,
        grid=(nb,),
        in_specs=[
            seq_block, seq_block, seq_block,
            pl.BlockSpec((rows, PAGE_SIZE), lambda b, pt: (0, 0)),
            pl.BlockSpec((rows, HEAD_DIM), lambda b, pt: (0, 0)),
            pl.BlockSpec(memory_space=pl.ANY),
            pl.BlockSpec(memory_space=pl.ANY),
        ],
        out_specs=pl.BlockSpec((1, rows, HEAD_DIM), lambda b, pt: (b, 0, 0)),
        scratch_shapes=[
            pltpu.VMEM((PAGE_SLOTS, PAGE_GROUP, PAGE_SIZE * N_HEADS, HEAD_DIM), F32),
            pltpu.VMEM((PAGE_SLOTS, PAGE_GROUP, PAGE_SIZE * N_HEADS, HEAD_DIM), F32),
            pltpu.SemaphoreType.DMA((2, PAGE_SLOTS, PAGE_GROUP)),
        ],
    )
    return pl.pallas_call(
        functools.partial(_attn_sample_kernel, n_pages=n_pages, dec_seq=dec_seq, n_seq=nb),
        grid_spec=grid_spec,
        out_shape=jax.ShapeDtypeStruct((nb, rows, HEAD_DIM), F32),
        compiler_params=_cparams(("arbitrary",)),
        name="attn_sample",
    )(page_ids.reshape(-1), q_s, k_s, v_s, bias_rows, go_rows, cache_k, cache_v)


def _conv_kernel(u_ref, hist_ref, w_ref, b_ref, g_ref, lb_ref, beta_ref, y_ref, win_ref, acc_ref, shift_ref,
                 *, tt):
    i = pl.program_id(1)

    @pl.when(i == 0)
    def _():
        win_ref[0:HALO, :] = hist_ref[0]

    @pl.when(i > 0)
    def _():
        win_ref[0:HALO, :] = win_ref[tt:tt + HALO, :]

    win_ref[HALO:HALO + tt, :] = u_ref[0]
    off = HALO - HIST
    def lane_block(cb, _):
        cs = pl.ds(pl.multiple_of(cb * LANES, LANES), LANES)
        acc = jnp.zeros((tt, LANES), F32)
        for s in range(SUBLANES):
            taps = [k for k in range(CONV_KERNEL) if (off + k) % SUBLANES == s]
            span = max(off + k for k in taps) - s + tt
            shift_ref[0:span, :] = win_ref[s:s + span, cs]
            for k in taps:
                base = off + k - s
                acc = acc + w_ref[k:k + 1, cs] * shift_ref[base:base + tt, :]
        acc_ref[:, cs] = acc
        return 0
    lax.fori_loop(0, CONV_CH // LANES, lane_block, 0)
    y = acc_ref[...] + b_ref[...]
    mu = jnp.mean(y, axis=-1, keepdims=True)
    yc = y - mu
    var = jnp.mean(yc * yc, axis=-1, keepdims=True)
    y = yc * lax.rsqrt(var + EPS) * g_ref[...] + lb_ref[...]
    y_ref[0] = (y * jax.nn.sigmoid(y) * beta_ref[...]).astype(BF16)


def _conv(u, hist, w_dw, b_dw, ln_g, ln_b, beta, tt):
    nb, s, c = u.shape
    row = pl.BlockSpec((1, c), lambda b, i: (0, 0))
    return pl.pallas_call(
        functools.partial(_conv_kernel, tt=tt),
        grid=(nb, s // tt),
        in_specs=[
            pl.BlockSpec((1, tt, c), lambda b, i: (b, i, 0)),
            pl.BlockSpec((1, HALO, c), lambda b, i: (b, 0, 0)),
            pl.BlockSpec((CONV_KERNEL, c), lambda b, i: (0, 0)),
            row, row, row, row,
        ],
        out_specs=pl.BlockSpec((1, tt, c), lambda b, i: (b, i, 0)),
        out_shape=jax.ShapeDtypeStruct((nb, s, c), BF16),
        scratch_shapes=[pltpu.VMEM((HALO + tt, c), F32), pltpu.VMEM((tt, c), F32),
                        pltpu.VMEM((HALO + tt, LANES), F32)],
        compiler_params=_cparams(("arbitrary", "arbitrary")),
        name="conv",
    )(u, hist, w_dw, b_dw, ln_g, ln_b, beta)


R_E1, R_E2, R_W1, R_W2, R_RANK1, R_RANK2 = range(6)
GRP_LANE0 = N_EXPERTS


def _lane_pick(x, lane, idx):
    return jnp.sum(jnp.where(lane == idx, x, 0.0), axis=-1, keepdims=True)


def _post_kernel(hp_ref, hs_ref, op_ref, os_ref, ycp_ref, ycs_ref, wt_ref, wb_ref, g_ref, wr_ref, br_ref,
                 h1_ref, xn_ref, route_ref, tot_ref, run_ref, *, n_prompt_tiles):
    i = pl.program_id(0)

    @pl.when(i == 0)
    def _():
        run_ref[...] = jnp.zeros_like(run_ref)

    is_prompt = i < n_prompt_tiles
    pick = lambda p_ref, s_ref: jnp.where(is_prompt, p_ref[...], s_ref[...])
    h1 = (pick(hp_ref, hs_ref)
          + jnp.dot(pick(op_ref, os_ref), wt_ref[...], preferred_element_type=F32)
          + jnp.dot(pick(ycp_ref, ycs_ref), wb_ref[...], preferred_element_type=F32))
    h1_ref[...] = h1
    ms = jnp.mean(h1 * h1, axis=-1, keepdims=True)
    xn = h1 * lax.rsqrt(ms + EPS) * g_ref[...]
    for cb in range(ROW_SLAB):
        xn_ref[pl.ds(cb, xn.shape[0], stride=ROW_SLAB), :] = xn[:, cb * LANES:(cb + 1) * LANES]

    x_hi = xn.astype(BF16)
    x_lo = (xn - x_hi.astype(F32)).astype(BF16)
    logits = (jnp.dot(x_hi, wr_ref[0], preferred_element_type=F32)
              + jnp.dot(x_lo, wr_ref[0], preferred_element_type=F32)
              + jnp.dot(x_hi, wr_ref[1], preferred_element_type=F32)) + br_ref[...]
    tm = logits.shape[0]
    lane = lax.broadcasted_iota(I32, (tm, LANES), 1)
    neg = -jnp.inf
    is_grp = (lane >= GRP_LANE0) & (lane < GRP_LANE0 + N_GROUPS)
    glog = jnp.where(is_grp, logits, neg)
    gmax = jnp.max(glog, axis=-1, keepdims=True)
    gidx = jnp.min(jnp.where(glog == gmax, lane - GRP_LANE0, LANES), axis=-1, keepdims=True)
    g_w = 1.0 / jnp.sum(jnp.exp(glog - gmax), axis=-1, keepdims=True)

    in_grp = (lane < N_EXPERTS) & (lane // EXPERTS_PER_GROUP == gidx)
    elog = jnp.where(in_grp, logits, neg)
    v1 = jnp.max(elog, axis=-1, keepdims=True)
    i1 = jnp.min(jnp.where(elog == v1, lane, LANES), axis=-1, keepdims=True)
    elog2 = jnp.where(lane == i1, neg, elog)
    v2 = jnp.max(elog2, axis=-1, keepdims=True)
    i2 = jnp.min(jnp.where(elog2 == v2, lane, LANES), axis=-1, keepdims=True)
    ew = jnp.exp(v2 - v1)
    w1 = g_w / (1.0 + ew)
    w2 = g_w * ew / (1.0 + ew)

    onehot = ((lane == i1) | (lane == i2)).astype(F32)
    trow = lax.broadcasted_iota(I32, (tm, tm), 0)
    tcol = lax.broadcasted_iota(I32, (tm, tm), 1)
    earlier = (tcol < trow).astype(BF16)
    before = run_ref[0:1, :] + jnp.dot(earlier, onehot.astype(BF16), preferred_element_type=F32)
    rank1 = _lane_pick(before, lane, i1)
    rank2 = _lane_pick(before, lane, i2)
    run_ref[...] = run_ref[...] + jnp.sum(onehot, axis=0, keepdims=True)
    tot_ref[...] = run_ref[...]

    rec = jnp.zeros((tm, LANES), F32)
    for idx, val in ((R_E1, i1.astype(F32)), (R_E2, i2.astype(F32)), (R_W1, w1), (R_W2, w2),
                     (R_RANK1, rank1), (R_RANK2, rank2)):
        rec = jnp.where(lane == idx, val, rec)
    route_ref[...] = rec


def _post(h_p, h_s, o_p, o_s, yc_p, yc_s, w_top, w_bot, g_ffn, w_router, b_router):
    npt = h_p.shape[0] // TOK_TM
    nst = h_s.shape[0] // TOK_TM
    t = (npt + nst) * TOK_TM
    tok = lambda w: pl.BlockSpec((TOK_TM, w), lambda i: (i, 0))
    full = lambda a: pl.BlockSpec(a.shape, lambda i: (0,) * a.ndim)
    prompt_tile = lambda w: pl.BlockSpec((TOK_TM, w), lambda i: (jnp.minimum(i, npt - 1), 0))
    sample_tile = lambda w: pl.BlockSpec((TOK_TM, w), lambda i: (jnp.clip(i - npt, 0, nst - 1), 0))
    return pl.pallas_call(
        functools.partial(_post_kernel, n_prompt_tiles=npt),
        grid=(npt + nst,),
        in_specs=[prompt_tile(D_MODEL), sample_tile(D_MODEL), prompt_tile(ATTN_WIDTH), sample_tile(ATTN_WIDTH),
                  prompt_tile(CONV_CH), sample_tile(CONV_CH), full(w_top), full(w_bot),
                  full(g_ffn), full(w_router), full(b_router)],
        out_specs=[tok(D_MODEL), pl.BlockSpec((TOK_TM * ROW_SLAB, LANES), lambda i: (i, 0)), tok(LANES),
                   pl.BlockSpec((8, LANES), lambda i: (0, 0))],
        out_shape=[jax.ShapeDtypeStruct((t, D_MODEL), F32), jax.ShapeDtypeStruct((t * ROW_SLAB, LANES), F32),
                   jax.ShapeDtypeStruct((t, LANES), F32), jax.ShapeDtypeStruct((8, LANES), F32)],
        scratch_shapes=[pltpu.VMEM((8, LANES), F32)],
        compiler_params=_cparams(("arbitrary",)),
        name="post",
    )(h_p, h_s, o_p, o_s, yc_p, yc_s, w_top, w_bot, g_ffn, w_router, b_router)


M_TILE_EXPERT, M_NUM_TILES = 0, 1


def _positions_kernel(route_ref, tot_ref, pos_ref, meta_ref):
    t = route_ref.shape[0]
    lane1 = lax.broadcasted_iota(I32, (8, LANES), 1)
    total = tot_ref[...]
    ntile = jnp.where(lane1 < N_EXPERTS, jnp.ceil(total / ROW_TM), 0.0)
    srow = lax.broadcasted_iota(I32, (LANES, LANES), 0)
    scol = lax.broadcasted_iota(I32, (LANES, LANES), 1)
    start_tile = jnp.dot(ntile.astype(BF16), (srow < scol).astype(BF16), preferred_element_type=F32)
    start_row = start_tile[0:1, :] * ROW_TM

    route = route_ref[...]
    lane = lax.broadcasted_iota(I32, (t, LANES), 1)
    e1 = _lane_pick(route, lane, R_E1).astype(I32)
    e2 = _lane_pick(route, lane, R_E2).astype(I32)
    p1 = _lane_pick(jnp.broadcast_to(start_row, (t, LANES)), lane, e1) + _lane_pick(route, lane, R_RANK1)
    p2 = _lane_pick(jnp.broadcast_to(start_row, (t, LANES)), lane, e2) + _lane_pick(route, lane, R_RANK2)
    pos_ref[...] = jnp.where(lane == 0, p1, jnp.where(lane == 1, p2, 0.0)).astype(I32)

    end_tile = start_tile[0:1, :] + ntile[0:1, :]
    end_col = jnp.sum(jnp.where(srow == scol, jnp.broadcast_to(end_tile, (LANES, LANES)), 0.0),
                      axis=-1, keepdims=True)
    done = ((end_col <= scol.astype(F32)) & (srow < N_EXPERTS)).astype(F32)
    tile_expert = jnp.minimum(jnp.sum(done, axis=0, keepdims=True), N_EXPERTS - 1.0)
    num_tiles = jnp.sum(ntile[0:1, :], axis=-1, keepdims=True)
    row8 = lax.broadcasted_iota(I32, (8, LANES), 0)
    meta = jnp.where(row8 == M_TILE_EXPERT, jnp.broadcast_to(tile_expert, (8, LANES)),
                     jnp.where(row8 == M_NUM_TILES, jnp.broadcast_to(num_tiles, (8, LANES)), 0.0))
    meta_ref[...] = meta.astype(I32)


def _positions(route, totals):
    t = route.shape[0]
    return pl.pallas_call(
        _positions_kernel,
        out_shape=[jax.ShapeDtypeStruct((t, LANES), I32), jax.ShapeDtypeStruct((8, LANES), I32)],
        compiler_params=pltpu.CompilerParams(vmem_limit_bytes=VMEM_LIMIT),
        name="positions",
    )(route, totals)


def _experts_kernel(te_ref, nt_ref, p1_ref, p2_ref, xn_hbm, wg_hbm, wu_hbm, wd_hbm,
                    y_ref, row_tok, run_ref, xg, wg_f32, wu_f32, wd_f32, wg_bf, wu_bf, wd_bf, sem, wsem,
                    *, n_tok, n_rows):
    j = pl.program_id(0)
    n_tiles = nt_ref[0]

    def weight_copies(expert, wslot):
        return (pltpu.make_async_copy(wg_hbm.at[expert], wg_f32.at[wslot], wsem.at[0, wslot]),
                pltpu.make_async_copy(wu_hbm.at[expert], wu_f32.at[wslot], wsem.at[1, wslot]),
                pltpu.make_async_copy(wd_hbm.at[expert], wd_f32.at[wslot], wsem.at[2, wslot]))

    def prefetch_next_expert(wslot):
        cur = te_ref[j]
        nxt = lax.while_loop(lambda t: (t < n_tiles) & (te_ref[t] == cur), lambda t: t + 1, j + 1)

        @pl.when(nxt < n_tiles)
        def _():
            for cp in weight_copies(te_ref[nxt], wslot):
                cp.start()

    def row_copy(r, slot, tok):
        src = xn_hbm.at[pl.ds(pl.multiple_of(tok * ROW_SLAB, ROW_SLAB), ROW_SLAB), :]
        dst = xg.at[slot, pl.ds(pl.multiple_of(r * ROW_SLAB, ROW_SLAB), ROW_SLAB), :]
        return pltpu.make_async_copy(src, dst, sem.at[slot])

    def gather(tile, slot):
        def issue(r, _):
            row_copy(r, slot, row_tok[tile * ROW_TM + r]).start()
            return 0
        lax.fori_loop(0, ROW_TM, issue, 0, unroll=DMA_UNROLL)

    @pl.when(j == 0)
    def _():
        def clear(r, _):
            row_tok[r] = 0
            return 0
        lax.fori_loop(0, n_rows, clear, 0, unroll=DMA_UNROLL)

        def place(t, _):
            row_tok[p1_ref[t]] = t
            row_tok[p2_ref[t]] = t
            return 0
        lax.fori_loop(0, n_tok, place, 0, unroll=DMA_UNROLL)
        gather(0, 0)
        run_ref[0] = 0
        for cp in weight_copies(te_ref[0], 0):
            cp.start()

    @pl.when(j < n_tiles)
    def _():
        slot = j % 2

        @pl.when(j + 1 < n_tiles)
        def _():
            gather(j + 1, 1 - slot)

        new_expert = (j == 0) | (te_ref[j] != te_ref[jnp.maximum(j - 1, 0)])

        @pl.when(new_expert & (j > 0))
        def _():
            run_ref[0] = run_ref[0] + 1

        @pl.when(new_expert)
        def _():
            wslot = run_ref[0] % 2
            for cp in weight_copies(0, wslot):
                cp.wait()
            prefetch_next_expert(1 - wslot)
            wg_bf[...] = wg_f32[wslot].astype(BF16)
            wu_bf[...] = wu_f32[wslot].astype(BF16)
            wd_bf[...] = wd_f32[wslot].astype(BF16)

        def drain(r, _):
            row_copy(r, slot, 0).wait()
            return 0
        lax.fori_loop(0, ROW_TM, drain, 0, unroll=DMA_UNROLL)

        x = jnp.concatenate([xg[slot, pl.ds(cb, ROW_TM, stride=ROW_SLAB), :].astype(BF16)
                             for cb in range(ROW_SLAB)], axis=1)
        hg = jnp.dot(x, wg_bf[...], preferred_element_type=F32)
        hu = jnp.dot(x, wu_bf[...], preferred_element_type=F32)
        hid = (hg * jax.nn.sigmoid(hg) * hu).astype(BF16)
        y_ref[...] = jnp.dot(hid, wd_bf[...], preferred_element_type=F32)

    @pl.when(j >= nt_ref[0])
    def _():
        y_ref[...] = jnp.zeros_like(y_ref)


def _experts(tile_expert, num_tiles, pos1, pos2, xn, w_gate, w_up, w_down):
    n_tok = xn.shape[0] // ROW_SLAB
    max_tiles = (2 * n_tok + N_EXPERTS * (ROW_TM - 1)) // ROW_TM
    n_rows = max_tiles * ROW_TM
    hbm = pl.BlockSpec(memory_space=pl.ANY)
    grid_spec = pltpu.PrefetchScalarGridSpec(
        num_scalar_prefetch=4,
        grid=(max_tiles,),
        in_specs=[hbm, hbm, hbm, hbm],
        out_specs=pl.BlockSpec((ROW_TM, D_MODEL), lambda j, te, nt, p1, p2: (j, 0)),
        scratch_shapes=[
            pltpu.SMEM((n_rows,), I32),
            pltpu.SMEM((1,), I32),
            pltpu.VMEM((2, ROW_TM * ROW_SLAB, LANES), F32),
            pltpu.VMEM((2, D_MODEL, D_EXPERT), F32),
            pltpu.VMEM((2, D_MODEL, D_EXPERT), F32),
            pltpu.VMEM((2, D_EXPERT, D_MODEL), F32),
            pltpu.VMEM((D_MODEL, D_EXPERT), BF16),
            pltpu.VMEM((D_MODEL, D_EXPERT), BF16),
            pltpu.VMEM((D_EXPERT, D_MODEL), BF16),
            pltpu.SemaphoreType.DMA((2,)),
            pltpu.SemaphoreType.DMA((3, 2)),
        ],
    )
    return pl.pallas_call(
        functools.partial(_experts_kernel, n_tok=n_tok, n_rows=n_rows),
        grid_spec=grid_spec,
        out_shape=jax.ShapeDtypeStruct((n_rows, D_MODEL), F32),
        compiler_params=_cparams(("arbitrary",)),
        name="experts",
    )(tile_expert, num_tiles, pos1, pos2, xn, w_gate, w_up, w_down)


def _final_kernel(p1_ref, p2_ref, h1_ref, route_ref, pp_ref, ps_ref, ys_hbm, g_ref, wpg_ref, wpp_ref,
                  outp_ref, outs_ref, y1, y2, sem, *, n_prompt_tiles):
    i = pl.program_id(0)
    slot = i % 2

    def row_copies(r, slot, pos1, pos2):
        return (pltpu.make_async_copy(ys_hbm.at[pl.ds(pos1, 1), :], y1.at[slot, pl.ds(r, 1), :], sem.at[0, slot]),
                pltpu.make_async_copy(ys_hbm.at[pl.ds(pos2, 1), :], y2.at[slot, pl.ds(r, 1), :], sem.at[1, slot]))

    def gather(tile, slot):
        def issue(r, _):
            tok = tile * TOK_TM + r
            for cp in row_copies(r, slot, p1_ref[tok], p2_ref[tok]):
                cp.start()
            return 0
        lax.fori_loop(0, TOK_TM, issue, 0, unroll=DMA_UNROLL)

    def drain(slot):
        def wait_row(r, _):
            for cp in row_copies(r, slot, 0, 0):
                cp.wait()
            return 0
        lax.fori_loop(0, TOK_TM, wait_row, 0, unroll=DMA_UNROLL)

    @pl.when(i == 0)
    def _():
        gather(0, 0)

    @pl.when(i + 1 < pl.num_programs(0))
    def _():
        gather(i + 1, 1 - slot)

    route = route_ref[...]
    lane = lax.broadcasted_iota(I32, route.shape, 1)
    w1 = _lane_pick(route, lane, R_W1)
    w2 = _lane_pick(route, lane, R_W2)
    p = jnp.where(i < n_prompt_tiles, pp_ref[...], ps_ref[...])
    emb = jnp.dot(p.astype(BF16), wpp_ref[...], preferred_element_type=F32)

    drain(slot)
    h2 = h1_ref[...] + w1 * y1[slot] + w2 * y2[slot]
    ms = jnp.mean(h2 * h2, axis=-1, keepdims=True)
    hn = (h2 * lax.rsqrt(ms + EPS) * g_ref[...]).astype(BF16)
    gate = jax.nn.sigmoid(jnp.dot(hn, wpg_ref[...], preferred_element_type=F32))
    res = h2 + gate * emb

    @pl.when(i < n_prompt_tiles)
    def _():
        outp_ref[...] = res

    @pl.when(i >= n_prompt_tiles)
    def _():
        outs_ref[...] = res


def _final(pos1, pos2, h1, route, p_p, p_s, ys, g_ple, w_ple_gate_bf, w_ple_proj_bf):
    t = h1.shape[0]
    n_prompt = p_p.shape[0]
    npt = n_prompt // TOK_TM
    nst = (t - n_prompt) // TOK_TM
    tok = lambda w: pl.BlockSpec((TOK_TM, w), lambda i, p1, p2: (i, 0))
    full = lambda a: pl.BlockSpec(a.shape, lambda i, p1, p2: (0,) * a.ndim)
    prompt_tile = lambda w: pl.BlockSpec((TOK_TM, w), lambda i, p1, p2: (jnp.minimum(i, npt - 1), 0))
    sample_tile = lambda w: pl.BlockSpec((TOK_TM, w), lambda i, p1, p2: (jnp.clip(i - npt, 0, nst - 1), 0))
    grid_spec = pltpu.PrefetchScalarGridSpec(
        num_scalar_prefetch=2,
        grid=(t // TOK_TM,),
        in_specs=[tok(D_MODEL), tok(LANES), prompt_tile(PLE_DIM), sample_tile(PLE_DIM),
                  pl.BlockSpec(memory_space=pl.ANY),
                  full(g_ple), full(w_ple_gate_bf), full(w_ple_proj_bf)],
        out_specs=[prompt_tile(D_MODEL), sample_tile(D_MODEL)],
        scratch_shapes=[pltpu.VMEM((2, TOK_TM, D_MODEL), F32), pltpu.VMEM((2, TOK_TM, D_MODEL), F32),
                        pltpu.SemaphoreType.DMA((2, 2))],
    )
    return pl.pallas_call(
        functools.partial(_final_kernel, n_prompt_tiles=npt),
        grid_spec=grid_spec,
        out_shape=[jax.ShapeDtypeStruct((n_prompt, D_MODEL), F32),
                   jax.ShapeDtypeStruct((t - n_prompt, D_MODEL), F32)],
        compiler_params=_cparams(("arbitrary",)),
        name="final",
    )(pos1, pos2, h1, route, p_p, p_s, ys, g_ple, w_ple_gate_bf, w_ple_proj_bf)


def _layer(i, hp, hs, cache_k, cache_v, state_conv, page_table, p_prompt, p_sample,
           g_mix_norm, w_in, g_q, g_k, sb_bias, g_o, w_dw, b_dw, ln_g, ln_b, beta_conv, w_out,
           g_ffn_norm, w_grp, b_grp, w_rt, b_rt, w_gate, w_up, w_down,
           g_ple_norm, w_ple_proj, w_ple_gate):
    batch, seq, _ = hp.shape
    nb, dec_seq, _ = hs.shape
    n_prompt = batch * seq
    n_sample = nb * dec_seq
    row = lambda a: a.reshape(1, -1)

    x_p = hp.reshape(n_prompt, D_MODEL)
    x_s = hs.reshape(n_sample, D_MODEL)
    mix_args = (row(g_mix_norm[i]), w_in[i].astype(BF16), row(g_q[i]), row(g_k[i]))
    q_p, k_p, v_p, kb_p, vb_p, u_p = _mix_in(x_p, *mix_args)
    q_s, k_s, v_s, _, _, u_s = _mix_in(x_s, *mix_args)

    o_p = _attn_prompt(sb_bias[i], q_p, kb_p, vb_p, row(g_o[i]), batch, seq)

    rows = N_HEADS * dec_seq
    split = lambda a: a.astype(F32).reshape(nb, dec_seq, ATTN_WIDTH)
    bias_rows = jnp.broadcast_to(jnp.repeat(sb_bias[i], dec_seq)[:, None], (rows, PAGE_SIZE))
    go_rows = jnp.repeat(g_o[i], dec_seq, axis=0)
    depth, n_pool = cache_k.shape[:2]
    o_s = _attn_sample(page_table + i * n_pool, split(q_s), split(k_s), split(v_s), bias_rows, go_rows,
                       cache_k.reshape(depth * n_pool, PAGE_SIZE * N_HEADS, HEAD_DIM),
                       cache_v.reshape(depth * n_pool, PAGE_SIZE * N_HEADS, HEAD_DIM))
    o_s = o_s.reshape(nb, N_HEADS, dec_seq, HEAD_DIM).transpose(0, 2, 1, 3).reshape(n_sample, ATTN_WIDTH)

    conv_args = (w_dw[i], row(b_dw[i]), row(ln_g[i]), row(ln_b[i]), row(beta_conv[i]))
    u_p = u_p.reshape(batch, seq, CONV_CH)
    u_s = u_s.reshape(nb, dec_seq, CONV_CH)
    hist_p = jnp.zeros((batch, HALO, CONV_CH), F32)
    hist_s = jnp.pad(state_conv[i], ((0, 0), (HALO - HIST, 0), (0, 0)))
    yc_p = _conv(u_p, hist_p, *conv_args, tt=CONV_TT)
    yc_s = _conv(u_s, hist_s, *conv_args, tt=dec_seq)

    w_out_bf = w_out[i].astype(BF16)
    w_router = jnp.zeros((D_MODEL, LANES), F32)
    w_router = w_router.at[:, :N_EXPERTS].set(w_rt[i]).at[:, GRP_LANE0:GRP_LANE0 + N_GROUPS].set(w_grp[i])
    b_router = jnp.zeros((1, LANES), F32)
    b_router = b_router.at[0, :N_EXPERTS].set(b_rt[i]).at[0, GRP_LANE0:GRP_LANE0 + N_GROUPS].set(b_grp[i])
    w_router_hi = w_router.astype(BF16)
    w_router_lo = (w_router - w_router_hi.astype(F32)).astype(BF16)
    h1, xn, route, totals = _post(x_p, x_s, o_p, o_s.astype(BF16), yc_p.reshape(n_prompt, CONV_CH),
                                  yc_s.reshape(n_sample, CONV_CH), w_out_bf[:ATTN_WIDTH], w_out_bf[ATTN_WIDTH:],
                                  row(g_ffn_norm[i]), jnp.stack([w_router_hi, w_router_lo]), b_router)

    pos, meta = _positions(route, totals)
    pos1, pos2 = pos[:, 0], pos[:, 1]
    ys = _experts(meta[M_TILE_EXPERT], meta[M_NUM_TILES, :1], pos1, pos2, xn, w_gate[i], w_up[i], w_down[i])

    y_p, y_s = _final(pos1, pos2, h1, route, p_prompt[i].reshape(n_prompt, PLE_DIM),
                      p_sample[i].reshape(n_sample, PLE_DIM), ys, row(g_ple_norm[i]),
                      w_ple_gate[i].astype(BF16), w_ple_proj[i].astype(BF16))

    heads = lambda a, b_, s_: a.reshape(b_, s_, N_HEADS, HEAD_DIM)
    conv_p = u_p[:, seq - HIST:]
    conv_s = jnp.concatenate([state_conv[i], u_s], axis=1)[:, -HIST:]
    return (y_p.reshape(batch, seq, D_MODEL), y_s.reshape(nb, dec_seq, D_MODEL),
            heads(k_p, batch, seq), heads(v_p, batch, seq), conv_p,
            heads(k_s, nb, dec_seq), heads(v_s, nb, dec_seq), conv_s)


def kernel(x_prompt, x_sample, cache_k, cache_v, state_conv, page_table, p_prompt, p_sample, g_mix_norm, w_in, g_q, g_k, sb_bias, g_o, w_dw, b_dw, ln_g, ln_b, beta_conv, w_out, g_ffn_norm, w_grp, b_grp, w_rt, b_rt, w_gate, w_up, w_down, g_ple_norm, w_ple_proj, w_ple_gate):
    depth = w_in.shape[0]
    hp, hs = x_prompt, x_sample
    outs = []
    for i in range(depth):
        res = _layer(i, hp, hs, cache_k, cache_v, state_conv, page_table, p_prompt, p_sample,
                     g_mix_norm, w_in, g_q, g_k, sb_bias, g_o, w_dw, b_dw, ln_g, ln_b, beta_conv, w_out,
                     g_ffn_norm, w_grp, b_grp, w_rt, b_rt, w_gate, w_up, w_down,
                     g_ple_norm, w_ple_proj, w_ple_gate)
        hp, hs = res[0], res[1]
        outs.append(res[2:])
    stacked = tuple(jnp.stack([o[n] for o in outs], axis=0) for n in range(6))
    return (hp, hs) + stacked
```

```python
import functools

import jax
import jax.numpy as jnp
from jax import lax
from jax.experimental import pallas as pl
from jax.experimental.pallas import tpu as pltpu

F32 = jnp.float32
BF16 = jnp.bfloat16
I32 = jnp.int32

D_MODEL = 2048
HEAD_DIM = 128
N_HEADS = 8
ATTN_WIDTH = N_HEADS * HEAD_DIM
CONV_CH = D_MODEL - ATTN_WIDTH
CONV_KERNEL = 31
HIST = CONV_KERNEL - 1
N_GROUPS = 4
EXPERTS_PER_GROUP = 8
N_EXPERTS = N_GROUPS * EXPERTS_PER_GROUP
D_EXPERT = D_MODEL // 4
PLE_DIM = 256
PAGE_SIZE = 128
EPS = 1e-6

LANES = 128
SUBLANES = 8
ROW_SLAB = D_MODEL // LANES
VMEM_LIMIT = 56 * 1024 * 1024

MIX_TM = 1024
MIX_TN = 512
ATT_T = 256
ATT_HEADS = 8
PAGE_GROUP = 4
PAGE_SLOTS = 3
CONV_TT = 256
HALO = 32
TOK_TM = 256
ROW_TM = 256
DMA_UNROLL = 8


def _cparams(sem):
    return pltpu.CompilerParams(dimension_semantics=sem, vmem_limit_bytes=VMEM_LIMIT)


def _log_survival(z):
    nz = -z
    return jnp.minimum(nz, 0.0) - jnp.log(1.0 + jnp.exp(jnp.minimum(z, nz)))


def _later_matrix(n):
    row = lax.broadcasted_iota(I32, (n, n), 0)
    col = lax.broadcasted_iota(I32, (n, n), 1)
    return (row > col).astype(BF16)


def _split_dot(l, m):
    hi = l.astype(BF16)
    lo = (l - hi.astype(F32)).astype(BF16)
    return (jnp.dot(hi, m, preferred_element_type=F32)
            + jnp.dot(lo, m, preferred_element_type=F32))


def _mix_kernel(x_ref, g_ref, wa_ref, wb_ref, gq_ref, gk_ref,
                q_ref, k_ref, v_ref, kb_ref, vb_ref, u_ref, xn_ref):
    j = pl.program_id(1)

    @pl.when(j == 0)
    def _():
        x = x_ref[...]
        ms = jnp.mean(x * x, axis=-1, keepdims=True)
        xn_ref[...] = (x * lax.rsqrt(ms + EPS) * g_ref[...]).astype(BF16)

    a = xn_ref[...]
    z = jnp.dot(a, wa_ref[...], preferred_element_type=F32)

    def head_norm(g):
        outs = []
        for hh in range(MIX_TN // HEAD_DIM):
            zh = z[:, hh * HEAD_DIM:(hh + 1) * HEAD_DIM]
            ms = jnp.mean(zh * zh, axis=-1, keepdims=True)
            outs.append(zh * lax.rsqrt(ms + EPS) * g)
        return jnp.concatenate(outs, axis=-1)

    @pl.when(j < 2)
    def _():
        q_ref[...] = (head_norm(gq_ref[...]) * (HEAD_DIM ** -0.5)).astype(BF16)

    @pl.when((j >= 2) & (j < 4))
    def _():
        kn = head_norm(gk_ref[...])
        k_ref[...] = kn
        kb_ref[...] = kn.astype(BF16)

    @pl.when((j >= 4) & (j < 6))
    def _():
        v_ref[...] = z
        vb_ref[...] = z.astype(BF16)

    @pl.when(j >= 6)
    def _():
        zb = jnp.dot(a, wb_ref[...], preferred_element_type=F32)
        u_ref[...] = z * jax.nn.sigmoid(zb)


def _mix_in(x, g_norm, w_in_bf, g_q, g_k):
    t = x.shape[0]
    tm = min(MIX_TM, t)
    nt = t // tm
    half = ATTN_WIDTH // MIX_TN

    def part_map(p):
        return lambda i, j: (i, jnp.clip(j - p * half, 0, half - 1))

    f32_out = jax.ShapeDtypeStruct((t, ATTN_WIDTH), F32)
    bf_out = jax.ShapeDtypeStruct((t, ATTN_WIDTH), BF16)
    out_block = lambda p: pl.BlockSpec((tm, MIX_TN), part_map(p))
    return pl.pallas_call(
        _mix_kernel,
        grid=(nt, 4 * half),
        in_specs=[
            pl.BlockSpec((tm, D_MODEL), lambda i, j: (i, 0)),
            pl.BlockSpec((1, D_MODEL), lambda i, j: (0, 0)),
            pl.BlockSpec((D_MODEL, MIX_TN), lambda i, j: (0, j)),
            pl.BlockSpec((D_MODEL, MIX_TN), lambda i, j: (0, jnp.maximum(j, 3 * half) + half)),
            pl.BlockSpec((1, HEAD_DIM), lambda i, j: (0, 0)),
            pl.BlockSpec((1, HEAD_DIM), lambda i, j: (0, 0)),
        ],
        out_specs=[out_block(0), out_block(1), out_block(2), out_block(1), out_block(2), out_block(3)],
        out_shape=[bf_out, f32_out, f32_out, bf_out, bf_out, f32_out],
        scratch_shapes=[pltpu.VMEM((tm, D_MODEL), BF16)],
        compiler_params=_cparams(("arbitrary", "arbitrary")),
        name="mix_in",
    )(x, g_norm, w_in_bf, w_in_bf, g_q, g_k)


def _attn_prompt_kernel(bias_ref, q_ref, k_ref, v_ref, go_ref, o_ref):
    hg = pl.program_id(1)
    i = pl.program_id(2)
    later = _later_matrix(ATT_T)
    heads = range(ATT_HEADS)
    hs = [slice(n * HEAD_DIM, (n + 1) * HEAD_DIM) for n in heads]
    bias = [bias_ref[hg * ATT_HEADS + n] for n in heads]
    qs = [q_ref[:, hs[n]] for n in heads]
    causal = (lax.broadcasted_iota(I32, (ATT_T, ATT_T), 1) < lax.broadcasted_iota(I32, (ATT_T, ATT_T), 0))

    def block(kb, carry, masked):
        ks = pl.multiple_of(kb * ATT_T, ATT_T)
        zs = [lax.dot_general(qs[n], k_ref[pl.ds(ks, ATT_T), hs[n]], (((1,), (1,)), ((), ())),
                              preferred_element_type=F32) + bias[n] for n in heads]
        ls = [_log_survival(z) for z in zs]
        lms = [jnp.where(causal, l, 0.0) for l in ls] if masked else ls
        rs = [_split_dot(lm, later) for lm in lms]
        avs = []
        for n in heads:
            a = jnp.exp(zs[n] + ls[n] + rs[n])
            if masked:
                a = jnp.where(causal, a, 0.0)
            avs.append(jnp.dot(a.astype(BF16), v_ref[pl.ds(ks, ATT_T), hs[n]], preferred_element_type=F32))
        return tuple((carry[n][0] + jnp.exp(carry[n][1]) * avs[n],
                      carry[n][1] + rs[n][:, 0:1] + lms[n][:, 0:1]) for n in heads)

    carry = tuple((jnp.zeros((ATT_T, HEAD_DIM), F32), jnp.zeros((ATT_T, 1), F32)) for _ in heads)
    carry = block(i, carry, True)
    carry = lax.fori_loop(0, i, lambda t, cr: block(i - 1 - t, cr, False), carry)
    for n in heads:
        o_acc = carry[n][0]
        ms = jnp.mean(o_acc * o_acc, axis=-1, keepdims=True)
        o_ref[:, hs[n]] = (o_acc * lax.rsqrt(ms + EPS) * go_ref[:, hs[n]]).astype(BF16)


def _attn_prompt(sb_bias, q_bf, kb, vb, g_o_row, batch, seq):
    nq = seq // ATT_T
    width = ATT_HEADS * HEAD_DIM
    return pl.pallas_call(
        _attn_prompt_kernel,
        grid=(batch, N_HEADS // ATT_HEADS, nq),
        in_specs=[
            pl.BlockSpec(memory_space=pltpu.SMEM),
            pl.BlockSpec((ATT_T, width), lambda b, h, i: (b * nq + i, h)),
            pl.BlockSpec((seq, width), lambda b, h, i: (b, h)),
            pl.BlockSpec((seq, width), lambda b, h, i: (b, h)),
            pl.BlockSpec((1, width), lambda b, h, i: (0, h)),
        ],
        out_specs=pl.BlockSpec((ATT_T, width), lambda b, h, i: (b * nq + i, h)),
        out_shape=jax.ShapeDtypeStruct((batch * seq, ATTN_WIDTH), BF16),
        compiler_params=_cparams(("arbitrary", "arbitrary", "arbitrary")),
        name="attn_prompt",
    )(sb_bias, q_bf, kb, vb, g_o_row)


def _attn_sample_kernel(pt_ref, q_ref, kn_ref, vn_ref, bias_ref, go_ref, ck_hbm, cv_hbm,
                        o_ref, kbuf, vbuf, sem, *, n_pages, dec_seq, n_seq):
    b = pl.program_id(0)
    rows = N_HEADS * dec_seq
    n_groups = n_pages // PAGE_GROUP

    def group_copies(seq, grp, slot):
        cps = []
        for g in range(PAGE_GROUP):
            page = pt_ref[seq * n_pages + n_pages - 1 - (grp * PAGE_GROUP + g)]
            cps.append(pltpu.make_async_copy(ck_hbm.at[page], kbuf.at[slot, g], sem.at[0, slot, g]))
            cps.append(pltpu.make_async_copy(cv_hbm.at[page], vbuf.at[slot, g], sem.at[1, slot, g]))
        return cps

    def start_group(n):
        @pl.when(n < n_seq * n_groups)
        def _():
            for cp in group_copies(n // n_groups, n % n_groups, n % PAGE_SLOTS):
                cp.start()

    @pl.when(b == 0)
    def _():
        for n in range(PAGE_SLOTS - 1):
            start_group(n)

    heads = range(N_HEADS)
    hs = [slice(h * HEAD_DIM, (h + 1) * HEAD_DIM) for h in heads]
    qr = [slice(h * dec_seq, (h + 1) * dec_seq) for h in heads]
    q = q_ref[0]
    q_h = [q[:, hs[h]].astype(BF16) for h in heads]
    later = _later_matrix(PAGE_SIZE)
    bias = bias_ref[...]

    def blocks(k_of, v_of, n_blocks, carry, mask):
        blks = range(n_blocks)
        zs = [jnp.concatenate(
            [lax.dot_general(q_h[h], k_of(blk, h).astype(BF16), (((1,), (1,)), ((), ())),
                             preferred_element_type=F32) for h in heads], axis=0) + bias for blk in blks]
        ls = [_log_survival(z) for z in zs]
        lms = ls if mask is None else [jnp.where(mask, l, 0.0) for l in ls]
        rs = [_split_dot(lm, later) for lm in lms]
        o_acc, c = carry
        for blk in blks:
            a = jnp.exp(zs[blk] + ls[blk] + rs[blk])
            if mask is not None:
                a = jnp.where(mask, a, 0.0)
            o_blk = jnp.concatenate(
                [jnp.dot(a[qr[h]].astype(BF16), v_of(blk, h).astype(BF16), preferred_element_type=F32)
                 for h in heads], axis=0)
            o_acc = o_acc + jnp.exp(c) * o_blk
            c = c + rs[blk][:, 0:1] + lms[blk][:, 0:1]
        return o_acc, c

    pad = jnp.zeros((PAGE_SIZE - dec_seq, HEAD_DIM), F32)
    k_new, v_new = kn_ref[0], vn_ref[0]
    qi = lax.broadcasted_iota(I32, (rows, PAGE_SIZE), 0) % dec_seq
    kj = lax.broadcasted_iota(I32, (rows, PAGE_SIZE), 1)
    carry = (jnp.zeros((rows, HEAD_DIM), F32), jnp.zeros((rows, 1), F32))
    carry = blocks(lambda blk, h: jnp.concatenate([k_new[:, hs[h]], pad], axis=0),
                   lambda blk, h: jnp.concatenate([v_new[:, hs[h]], pad], axis=0), 1, carry, kj < qi)

    def group_step(t, carry):
        n = b * n_groups + t
        slot = n % PAGE_SLOTS
        for cp in group_copies(0, 0, slot):
            cp.wait()
        start_group(n + PAGE_SLOTS - 1)

        def head_rows(buf):
            return lambda g, h: buf[slot, g, pl.ds(h, PAGE_SIZE, stride=N_HEADS), :]
        return blocks(head_rows(kbuf), head_rows(vbuf), PAGE_GROUP, carry, None)

    o_acc, _ = lax.fori_loop(0, n_groups, group_step, carry)
    ms = jnp.mean(o_acc * o_acc, axis=-1, keepdims=True)
    o_ref[0] = o_acc * lax.rsqrt(ms + EPS) * go_ref[...]


def _attn_sample(page_ids, q_s, k_s, v_s, bias_rows, go_rows, cache_k, cache_v):
    nb, dec_seq, _ = q_s.shape
    n_pages = page_ids.shape[1]
    rows = N_HEADS * dec_seq
    seq_block = pl.BlockSpec((1, dec_seq, ATTN_WIDTH), lambda b, pt: (b, 0, 0))
    grid_spec = pltpu.PrefetchScalarGridSpec(
        num_scalar_prefetch=1,
        grid=(nb,),
        in_specs=[
            seq_block, seq_block, seq_block,
            pl.BlockSpec((rows, PAGE_SIZE), lambda b, pt: (0, 0)),
            pl.BlockSpec((rows, HEAD_DIM), lambda b, pt: (0, 0)),
            pl.BlockSpec(memory_space=pl.ANY),
            pl.BlockSpec(memory_space=pl.ANY),
        ],
        out_specs=pl.BlockSpec((1, rows, HEAD_DIM), lambda b, pt: (b, 0, 0)),
        scratch_shapes=[
            pltpu.VMEM((PAGE_SLOTS, PAGE_GROUP, PAGE_SIZE * N_HEADS, HEAD_DIM), F32),
            pltpu.VMEM((PAGE_SLOTS, PAGE_GROUP, PAGE_SIZE * N_HEADS, HEAD_DIM), F32),
            pltpu.SemaphoreType.DMA((2, PAGE_SLOTS, PAGE_GROUP)),
        ],
    )
    return pl.pallas_call(
        functools.partial(_attn_sample_kernel, n_pages=n_pages, dec_seq=dec_seq, n_seq=nb),
        grid_spec=grid_spec,
        out_shape=jax.ShapeDtypeStruct((nb, rows, HEAD_DIM), F32),
        compiler_params=_cparams(("arbitrary",)),
        name="attn_sample",
    )(page_ids.reshape(-1), q_s, k_s, v_s, bias_rows, go_rows, cache_k, cache_v)


def _conv_kernel(u_ref, hist_ref, w_ref, b_ref, g_ref, lb_ref, beta_ref, y_ref, win_ref, acc_ref, shift_ref,
                 *, tt):
    i = pl.program_id(1)

    @pl.when(i == 0)
    def _():
        win_ref[0:HALO, :] = hist_ref[0]

    @pl.when(i > 0)
    def _():
        win_ref[0:HALO, :] = win_ref[tt:tt + HALO, :]

    win_ref[HALO:HALO + tt, :] = u_ref[0]
    off = HALO - HIST
    def lane_block(cb, _):
        cs = pl.ds(pl.multiple_of(cb * LANES, LANES), LANES)
        acc = jnp.zeros((tt, LANES), F32)
        for s in range(SUBLANES):
            taps = [k for k in range(CONV_KERNEL) if (off + k) % SUBLANES == s]
            span = max(off + k for k in taps) - s + tt
            shift_ref[0:span, :] = win_ref[s:s + span, cs]
            for k in taps:
                base = off + k - s
                acc = acc + w_ref[k:k + 1, cs] * shift_ref[base:base + tt, :]
        acc_ref[:, cs] = acc
        return 0
    lax.fori_loop(0, CONV_CH // LANES, lane_block, 0)
    y = acc_ref[...] + b_ref[...]
    mu = jnp.mean(y, axis=-1, keepdims=True)
    yc = y - mu
    var = jnp.mean(yc * yc, axis=-1, keepdims=True)
    y = yc * lax.rsqrt(var + EPS) * g_ref[...] + lb_ref[...]
    y_ref[0] = (y * jax.nn.sigmoid(y) * beta_ref[...]).astype(BF16)


def _conv(u, hist, w_dw, b_dw, ln_g, ln_b, beta, tt):
    nb, s, c = u.shape
    row = pl.BlockSpec((1, c), lambda b, i: (0, 0))
    return pl.pallas_call(
        functools.partial(_conv_kernel, tt=tt),
        grid=(nb, s // tt),
        in_specs=[
            pl.BlockSpec((1, tt, c), lambda b, i: (b, i, 0)),
            pl.BlockSpec((1, HALO, c), lambda b, i: (b, 0, 0)),
            pl.BlockSpec((CONV_KERNEL, c), lambda b, i: (0, 0)),
            row, row, row, row,
        ],
        out_specs=pl.BlockSpec((1, tt, c), lambda b, i: (b, i, 0)),
        out_shape=jax.ShapeDtypeStruct((nb, s, c), BF16),
        scratch_shapes=[pltpu.VMEM((HALO + tt, c), F32), pltpu.VMEM((tt, c), F32),
                        pltpu.VMEM((HALO + tt, LANES), F32)],
        compiler_params=_cparams(("arbitrary", "arbitrary")),
        name="conv",
    )(u, hist, w_dw, b_dw, ln_g, ln_b, beta)


R_E1, R_E2, R_W1, R_W2, R_RANK1, R_RANK2 = range(6)
GRP_LANE0 = N_EXPERTS


def _lane_pick(x, lane, idx):
    return jnp.sum(jnp.where(lane == idx, x, 0.0), axis=-1, keepdims=True)


def _post_kernel(hp_ref, hs_ref, op_ref, os_ref, ycp_ref, ycs_ref, wt_ref, wb_ref, g_ref, wr_ref, br_ref,
                 h1_ref, xn_ref, route_ref, tot_ref, run_ref, *, n_prompt_tiles):
    i = pl.program_id(0)

    @pl.when(i == 0)
    def _():
        run_ref[...] = jnp.zeros_like(run_ref)

    is_prompt = i < n_prompt_tiles
    pick = lambda p_ref, s_ref: jnp.where(is_prompt, p_ref[...], s_ref[...])
    h1 = (pick(hp_ref, hs_ref)
          + jnp.dot(pick(op_ref, os_ref), wt_ref[...], preferred_element_type=F32)
          + jnp.dot(pick(ycp_ref, ycs_ref), wb_ref[...], preferred_element_type=F32))
    h1_ref[...] = h1
    ms = jnp.mean(h1 * h1, axis=-1, keepdims=True)
    xn = h1 * lax.rsqrt(ms + EPS) * g_ref[...]
    for cb in range(ROW_SLAB):
        xn_ref[pl.ds(cb, xn.shape[0], stride=ROW_SLAB), :] = xn[:, cb * LANES:(cb + 1) * LANES]

    x_hi = xn.astype(BF16)
    x_lo = (xn - x_hi.astype(F32)).astype(BF16)
    logits = (jnp.dot(x_hi, wr_ref[0], preferred_element_type=F32)
              + jnp.dot(x_lo, wr_ref[0], preferred_element_type=F32)
              + jnp.dot(x_hi, wr_ref[1], preferred_element_type=F32)) + br_ref[...]
    tm = logits.shape[0]
    lane = lax.broadcasted_iota(I32, (tm, LANES), 1)
    neg = -jnp.inf
    is_grp = (lane >= GRP_LANE0) & (lane < GRP_LANE0 + N_GROUPS)
    glog = jnp.where(is_grp, logits, neg)
    gmax = jnp.max(glog, axis=-1, keepdims=True)
    gidx = jnp.min(jnp.where(glog == gmax, lane - GRP_LANE0, LANES), axis=-1, keepdims=True)
    g_w = 1.0 / jnp.sum(jnp.exp(glog - gmax), axis=-1, keepdims=True)

    in_grp = (lane < N_EXPERTS) & (lane // EXPERTS_PER_GROUP == gidx)
    elog = jnp.where(in_grp, logits, neg)
    v1 = jnp.max(elog, axis=-1, keepdims=True)
    i1 = jnp.min(jnp.where(elog == v1, lane, LANES), axis=-1, keepdims=True)
    elog2 = jnp.where(lane == i1, neg, elog)
    v2 = jnp.max(elog2, axis=-1, keepdims=True)
    i2 = jnp.min(jnp.where(elog2 == v2, lane, LANES), axis=-1, keepdims=True)
    ew = jnp.exp(v2 - v1)
    w1 = g_w / (1.0 + ew)
    w2 = g_w * ew / (1.0 + ew)

    onehot = ((lane == i1) | (lane == i2)).astype(F32)
    trow = lax.broadcasted_iota(I32, (tm, tm), 0)
    tcol = lax.broadcasted_iota(I32, (tm, tm), 1)
    earlier = (tcol < trow).astype(BF16)
    before = run_ref[0:1, :] + jnp.dot(earlier, onehot.astype(BF16), preferred_element_type=F32)
    rank1 = _lane_pick(before, lane, i1)
    rank2 = _lane_pick(before, lane, i2)
    run_ref[...] = run_ref[...] + jnp.sum(onehot, axis=0, keepdims=True)
    tot_ref[...] = run_ref[...]

    rec = jnp.zeros((tm, LANES), F32)
    for idx, val in ((R_E1, i1.astype(F32)), (R_E2, i2.astype(F32)), (R_W1, w1), (R_W2, w2),
                     (R_RANK1, rank1), (R_RANK2, rank2)):
        rec = jnp.where(lane == idx, val, rec)
    route_ref[...] = rec


def _post(h_p, h_s, o_p, o_s, yc_p, yc_s, w_top, w_bot, g_ffn, w_router, b_router):
    npt = h_p.shape[0] // TOK_TM
    nst = h_s.shape[0] // TOK_TM
    t = (npt + nst) * TOK_TM
    tok = lambda w: pl.BlockSpec((TOK_TM, w), lambda i: (i, 0))
    full = lambda a: pl.BlockSpec(a.shape, lambda i: (0,) * a.ndim)
    prompt_tile = lambda w: pl.BlockSpec((TOK_TM, w), lambda i: (jnp.minimum(i, npt - 1), 0))
    sample_tile = lambda w: pl.BlockSpec((TOK_TM, w), lambda i: (jnp.clip(i - npt, 0, nst - 1), 0))
    return pl.pallas_call(
        functools.partial(_post_kernel, n_prompt_tiles=npt),
        grid=(npt + nst,),
        in_specs=[prompt_tile(D_MODEL), sample_tile(D_MODEL), prompt_tile(ATTN_WIDTH), sample_tile(ATTN_WIDTH),
                  prompt_tile(CONV_CH), sample_tile(CONV_CH), full(w_top), full(w_bot),
                  full(g_ffn), full(w_router), full(b_router)],
        out_specs=[tok(D_MODEL), pl.BlockSpec((TOK_TM * ROW_SLAB, LANES), lambda i: (i, 0)), tok(LANES),
                   pl.BlockSpec((8, LANES), lambda i: (0, 0))],
        out_shape=[jax.ShapeDtypeStruct((t, D_MODEL), F32), jax.ShapeDtypeStruct((t * ROW_SLAB, LANES), F32),
                   jax.ShapeDtypeStruct((t, LANES), F32), jax.ShapeDtypeStruct((8, LANES), F32)],
        scratch_shapes=[pltpu.VMEM((8, LANES), F32)],
        compiler_params=_cparams(("arbitrary",)),
        name="post",
    )(h_p, h_s, o_p, o_s, yc_p, yc_s, w_top, w_bot, g_ffn, w_router, b_router)


M_TILE_EXPERT, M_NUM_TILES = 0, 1


def _positions_kernel(route_ref, tot_ref, pos_ref, meta_ref):
    t = route_ref.shape[0]
    lane1 = lax.broadcasted_iota(I32, (8, LANES), 1)
    total = tot_ref[...]
    ntile = jnp.where(lane1 < N_EXPERTS, jnp.ceil(total / ROW_TM), 0.0)
    srow = lax.broadcasted_iota(I32, (LANES, LANES), 0)
    scol = lax.broadcasted_iota(I32, (LANES, LANES), 1)
    start_tile = jnp.dot(ntile.astype(BF16), (srow < scol).astype(BF16), preferred_element_type=F32)
    start_row = start_tile[0:1, :] * ROW_TM

    route = route_ref[...]
    lane = lax.broadcasted_iota(I32, (t, LANES), 1)
    e1 = _lane_pick(route, lane, R_E1).astype(I32)
    e2 = _lane_pick(route, lane, R_E2).astype(I32)
    p1 = _lane_pick(jnp.broadcast_to(start_row, (t, LANES)), lane, e1) + _lane_pick(route, lane, R_RANK1)
    p2 = _lane_pick(jnp.broadcast_to(start_row, (t, LANES)), lane, e2) + _lane_pick(route, lane, R_RANK2)
    pos_ref[...] = jnp.where(lane == 0, p1, jnp.where(lane == 1, p2, 0.0)).astype(I32)

    end_tile = start_tile[0:1, :] + ntile[0:1, :]
    end_col = jnp.sum(jnp.where(srow == scol, jnp.broadcast_to(end_tile, (LANES, LANES)), 0.0),
                      axis=-1, keepdims=True)
    done = ((end_col <= scol.astype(F32)) & (srow < N_EXPERTS)).astype(F32)
    tile_expert = jnp.minimum(jnp.sum(done, axis=0, keepdims=True), N_EXPERTS - 1.0)
    num_tiles = jnp.sum(ntile[0:1, :], axis=-1, keepdims=True)
    row8 = lax.broadcasted_iota(I32, (8, LANES), 0)
    meta = jnp.where(row8 == M_TILE_EXPERT, jnp.broadcast_to(tile_expert, (8, LANES)),
                     jnp.where(row8 == M_NUM_TILES, jnp.broadcast_to(num_tiles, (8, LANES)), 0.0))
    meta_ref[...] = meta.astype(I32)


def _positions(route, totals):
    t = route.shape[0]
    return pl.pallas_call(
        _positions_kernel,
        out_shape=[jax.ShapeDtypeStruct((t, LANES), I32), jax.ShapeDtypeStruct((8, LANES), I32)],
        compiler_params=pltpu.CompilerParams(vmem_limit_bytes=VMEM_LIMIT),
        name="positions",
    )(route, totals)


def _experts_kernel(te_ref, nt_ref, p1_ref, p2_ref, xn_hbm, wg_hbm, wu_hbm, wd_hbm,
                    y_ref, row_tok, run_ref, xg, wg_f32, wu_f32, wd_f32, wg_bf, wu_bf, wd_bf, sem, wsem,
                    *, n_tok, n_rows):
    j = pl.program_id(0)
    n_tiles = nt_ref[0]

    def weight_copies(expert, wslot):
        return (pltpu.make_async_copy(wg_hbm.at[expert], wg_f32.at[wslot], wsem.at[0, wslot]),
                pltpu.make_async_copy(wu_hbm.at[expert], wu_f32.at[wslot], wsem.at[1, wslot]),
                pltpu.make_async_copy(wd_hbm.at[expert], wd_f32.at[wslot], wsem.at[2, wslot]))

    def prefetch_next_expert(wslot):
        cur = te_ref[j]
        nxt = lax.while_loop(lambda t: (t < n_tiles) & (te_ref[t] == cur), lambda t: t + 1, j + 1)

        @pl.when(nxt < n_tiles)
        def _():
            for cp in weight_copies(te_ref[nxt], wslot):
                cp.start(priority=1)

    def row_copy(r, slot, tok):
        src = xn_hbm.at[pl.ds(pl.multiple_of(tok * ROW_SLAB, ROW_SLAB), ROW_SLAB), :]
        dst = xg.at[slot, pl.ds(pl.multiple_of(r * ROW_SLAB, ROW_SLAB), ROW_SLAB), :]
        return pltpu.make_async_copy(src, dst, sem.at[slot])

    def gather(tile, slot):
        def issue(r, _):
            row_copy(r, slot, row_tok[tile * ROW_TM + r]).start()
            return 0
        lax.fori_loop(0, ROW_TM, issue, 0, unroll=DMA_UNROLL)

    @pl.when(j == 0)
    def _():
        def clear(r, _):
            row_tok[r] = 0
            return 0
        lax.fori_loop(0, n_rows, clear, 0, unroll=DMA_UNROLL)

        def place(t, _):
            row_tok[p1_ref[t]] = t
            row_tok[p2_ref[t]] = t
            return 0
        lax.fori_loop(0, n_tok, place, 0, unroll=DMA_UNROLL)
        gather(0, 0)
        run_ref[0] = 0
        for cp in weight_copies(te_ref[0], 0):
            cp.start()

    @pl.when(j < n_tiles)
    def _():
        slot = j % 2

        @pl.when(j + 1 < n_tiles)
        def _():
            gather(j + 1, 1 - slot)

        new_expert = (j == 0) | (te_ref[j] != te_ref[jnp.maximum(j - 1, 0)])

        @pl.when(new_expert & (j > 0))
        def _():
            run_ref[0] = run_ref[0] + 1

        @pl.when(new_expert)
        def _():
            wslot = run_ref[0] % 2
            for cp in weight_copies(0, wslot):
                cp.wait()
            prefetch_next_expert(1 - wslot)
            wg_bf[...] = wg_f32[wslot].astype(BF16)
            wu_bf[...] = wu_f32[wslot].astype(BF16)
            wd_bf[...] = wd_f32[wslot].astype(BF16)

        def drain(r, _):
            row_copy(r, slot, 0).wait()
            return 0
        lax.fori_loop(0, ROW_TM, drain, 0, unroll=DMA_UNROLL)

        x = jnp.concatenate([xg[slot, pl.ds(cb, ROW_TM, stride=ROW_SLAB), :].astype(BF16)
                             for cb in range(ROW_SLAB)], axis=1)
        hg = jnp.dot(x, wg_bf[...], preferred_element_type=F32)
        hu = jnp.dot(x, wu_bf[...], preferred_element_type=F32)
        hid = (hg * jax.nn.sigmoid(hg) * hu).astype(BF16)
        y_ref[...] = jnp.dot(hid, wd_bf[...], preferred_element_type=F32)

    @pl.when(j >= nt_ref[0])
    def _():
        y_ref[...] = jnp.zeros_like(y_ref)


def _experts(tile_expert, num_tiles, pos1, pos2, xn, w_gate, w_up, w_down):
    n_tok = xn.shape[0] // ROW_SLAB
    max_tiles = (2 * n_tok + N_EXPERTS * (ROW_TM - 1)) // ROW_TM
    n_rows = max_tiles * ROW_TM
    hbm = pl.BlockSpec(memory_space=pl.ANY)
    grid_spec = pltpu.PrefetchScalarGridSpec(
        num_scalar_prefetch=4,
        grid=(max_tiles,),
        in_specs=[hbm, hbm, hbm, hbm],
        out_specs=pl.BlockSpec((ROW_TM, D_MODEL), lambda j, te, nt, p1, p2: (j, 0)),
        scratch_shapes=[
            pltpu.SMEM((n_rows,), I32),
            pltpu.SMEM((1,), I32),
            pltpu.VMEM((2, ROW_TM * ROW_SLAB, LANES), F32),
            pltpu.VMEM((2, D_MODEL, D_EXPERT), F32),
            pltpu.VMEM((2, D_MODEL, D_EXPERT), F32),
            pltpu.VMEM((2, D_EXPERT, D_MODEL), F32),
            pltpu.VMEM((D_MODEL, D_EXPERT), BF16),
            pltpu.VMEM((D_MODEL, D_EXPERT), BF16),
            pltpu.VMEM((D_EXPERT, D_MODEL), BF16),
            pltpu.SemaphoreType.DMA((2,)),
            pltpu.SemaphoreType.DMA((3, 2)),
        ],
    )
    return pl.pallas_call(
        functools.partial(_experts_kernel, n_tok=n_tok, n_rows=n_rows),
        grid_spec=grid_spec,
        out_shape=jax.ShapeDtypeStruct((n_rows, D_MODEL), F32),
        compiler_params=_cparams(("arbitrary",)),
        name="experts",
    )(tile_expert, num_tiles, pos1, pos2, xn, w_gate, w_up, w_down)


def _final_kernel(p1_ref, p2_ref, h1_ref, route_ref, pp_ref, ps_ref, ys_hbm, g_ref, wpg_ref, wpp_ref,
                  outp_ref, outs_ref, y1, y2, sem, *, n_prompt_tiles):
    i = pl.program_id(0)
    slot = i % 2

    def row_copies(r, slot, pos1, pos2):
        return (pltpu.make_async_copy(ys_hbm.at[pl.ds(pos1, 1), :], y1.at[slot, pl.ds(r, 1), :], sem.at[0, slot]),
                pltpu.make_async_copy(ys_hbm.at[pl.ds(pos2, 1), :], y2.at[slot, pl.ds(r, 1), :], sem.at[1, slot]))

    def gather(tile, slot):
        def issue(r, _):
            tok = tile * TOK_TM + r
            for cp in row_copies(r, slot, p1_ref[tok], p2_ref[tok]):
                cp.start()
            return 0
        lax.fori_loop(0, TOK_TM, issue, 0, unroll=DMA_UNROLL)

    def drain(slot):
        def wait_row(r, _):
            for cp in row_copies(r, slot, 0, 0):
                cp.wait()
            return 0
        lax.fori_loop(0, TOK_TM, wait_row, 0, unroll=DMA_UNROLL)

    @pl.when(i == 0)
    def _():
        gather(0, 0)

    @pl.when(i + 1 < pl.num_programs(0))
    def _():
        gather(i + 1, 1 - slot)

    route = route_ref[...]
    lane = lax.broadcasted_iota(I32, route.shape, 1)
    w1 = _lane_pick(route, lane, R_W1)
    w2 = _lane_pick(route, lane, R_W2)
    p = jnp.where(i < n_prompt_tiles, pp_ref[...], ps_ref[...])
    emb = jnp.dot(p.astype(BF16), wpp_ref[...], preferred_element_type=F32)

    drain(slot)
    h2 = h1_ref[...] + w1 * y1[slot] + w2 * y2[slot]
    ms = jnp.mean(h2 * h2, axis=-1, keepdims=True)
    hn = (h2 * lax.rsqrt(ms + EPS) * g_ref[...]).astype(BF16)
    gate = jax.nn.sigmoid(jnp.dot(hn, wpg_ref[...], preferred_element_type=F32))
    res = h2 + gate * emb

    @pl.when(i < n_prompt_tiles)
    def _():
        outp_ref[...] = res

    @pl.when(i >= n_prompt_tiles)
    def _():
        outs_ref[...] = res


def _final(pos1, pos2, h1, route, p_p, p_s, ys, g_ple, w_ple_gate_bf, w_ple_proj_bf):
    t = h1.shape[0]
    n_prompt = p_p.shape[0]
    npt = n_prompt // TOK_TM
    nst = (t - n_prompt) // TOK_TM
    tok = lambda w: pl.BlockSpec((TOK_TM, w), lambda i, p1, p2: (i, 0))
    full = lambda a: pl.BlockSpec(a.shape, lambda i, p1, p2: (0,) * a.ndim)
    prompt_tile = lambda w: pl.BlockSpec((TOK_TM, w), lambda i, p1, p2: (jnp.minimum(i, npt - 1), 0))
    sample_tile = lambda w: pl.BlockSpec((TOK_TM, w), lambda i, p1, p2: (jnp.clip(i - npt, 0, nst - 1), 0))
    grid_spec = pltpu.PrefetchScalarGridSpec(
        num_scalar_prefetch=2,
        grid=(t // TOK_TM,),
        in_specs=[tok(D_MODEL), tok(LANES), prompt_tile(PLE_DIM), sample_tile(PLE_DIM),
                  pl.BlockSpec(memory_space=pl.ANY),
                  full(g_ple), full(w_ple_gate_bf), full(w_ple_proj_bf)],
        out_specs=[prompt_tile(D_MODEL), sample_tile(D_MODEL)],
        scratch_shapes=[pltpu.VMEM((2, TOK_TM, D_MODEL), F32), pltpu.VMEM((2, TOK_TM, D_MODEL), F32),
                        pltpu.SemaphoreType.DMA((2, 2))],
    )
    return pl.pallas_call(
        functools.partial(_final_kernel, n_prompt_tiles=npt),
        grid_spec=grid_spec,
        out_shape=[jax.ShapeDtypeStruct((n_prompt, D_MODEL), F32),
                   jax.ShapeDtypeStruct((t - n_prompt, D_MODEL), F32)],
        compiler_params=_cparams(("arbitrary",)),
        name="final",
    )(pos1, pos2, h1, route, p_p, p_s, ys, g_ple, w_ple_gate_bf, w_ple_proj_bf)


def _layer(i, hp, hs, cache_k, cache_v, state_conv, page_table, p_prompt, p_sample,
           g_mix_norm, w_in, g_q, g_k, sb_bias, g_o, w_dw, b_dw, ln_g, ln_b, beta_conv, w_out,
           g_ffn_norm, w_grp, b_grp, w_rt, b_rt, w_gate, w_up, w_down,
           g_ple_norm, w_ple_proj, w_ple_gate):
    batch, seq, _ = hp.shape
    nb, dec_seq, _ = hs.shape
    n_prompt = batch * seq
    n_sample = nb * dec_seq
    row = lambda a: a.reshape(1, -1)

    x_p = hp.reshape(n_prompt, D_MODEL)
    x_s = hs.reshape(n_sample, D_MODEL)
    mix_args = (row(g_mix_norm[i]), w_in[i].astype(BF16), row(g_q[i]), row(g_k[i]))
    q_p, k_p, v_p, kb_p, vb_p, u_p = _mix_in(x_p, *mix_args)
    q_s, k_s, v_s, _, _, u_s = _mix_in(x_s, *mix_args)

    o_p = _attn_prompt(sb_bias[i], q_p, kb_p, vb_p, row(g_o[i]), batch, seq)

    rows = N_HEADS * dec_seq
    split = lambda a: a.astype(F32).reshape(nb, dec_seq, ATTN_WIDTH)
    bias_rows = jnp.broadcast_to(jnp.repeat(sb_bias[i], dec_seq)[:, None], (rows, PAGE_SIZE))
    go_rows = jnp.repeat(g_o[i], dec_seq, axis=0)
    depth, n_pool = cache_k.shape[:2]
    o_s = _attn_sample(page_table + i * n_pool, split(q_s), split(k_s), split(v_s), bias_rows, go_rows,
                       cache_k.reshape(depth * n_pool, PAGE_SIZE * N_HEADS, HEAD_DIM),
                       cache_v.reshape(depth * n_pool, PAGE_SIZE * N_HEADS, HEAD_DIM))
    o_s = o_s.reshape(nb, N_HEADS, dec_seq, HEAD_DIM).transpose(0, 2, 1, 3).reshape(n_sample, ATTN_WIDTH)

    conv_args = (w_dw[i], row(b_dw[i]), row(ln_g[i]), row(ln_b[i]), row(beta_conv[i]))
    u_p = u_p.reshape(batch, seq, CONV_CH)
    u_s = u_s.reshape(nb, dec_seq, CONV_CH)
    hist_p = jnp.zeros((batch, HALO, CONV_CH), F32)
    hist_s = jnp.pad(state_conv[i], ((0, 0), (HALO - HIST, 0), (0, 0)))
    yc_p = _conv(u_p, hist_p, *conv_args, tt=CONV_TT)
    yc_s = _conv(u_s, hist_s, *conv_args, tt=dec_seq)

    w_out_bf = w_out[i].astype(BF16)
    w_router = jnp.zeros((D_MODEL, LANES), F32)
    w_router = w_router.at[:, :N_EXPERTS].set(w_rt[i]).at[:, GRP_LANE0:GRP_LANE0 + N_GROUPS].set(w_grp[i])
    b_router = jnp.zeros((1, LANES), F32)
    b_router = b_router.at[0, :N_EXPERTS].set(b_rt[i]).at[0, GRP_LANE0:GRP_LANE0 + N_GROUPS].set(b_grp[i])
    w_router_hi = w_router.astype(BF16)
    w_router_lo = (w_router - w_router_hi.astype(F32)).astype(BF16)
    h1, xn, route, totals = _post(x_p, x_s, o_p, o_s.astype(BF16), yc_p.reshape(n_prompt, CONV_CH),
                                  yc_s.reshape(n_sample, CONV_CH), w_out_bf[:ATTN_WIDTH], w_out_bf[ATTN_WIDTH:],
                                  row(g_ffn_norm[i]), jnp.stack([w_router_hi, w_router_lo]), b_router)

    pos, meta = _positions(route, totals)
    pos1, pos2 = pos[:, 0], pos[:, 1]
    ys = _experts(meta[M_TILE_EXPERT], meta[M_NUM_TILES, :1], pos1, pos2, xn, w_gate[i], w_up[i], w_down[i])

    y_p, y_s = _final(pos1, pos2, h1, route, p_prompt[i].reshape(n_prompt, PLE_DIM),
                      p_sample[i].reshape(n_sample, PLE_DIM), ys, row(g_ple_norm[i]),
                      w_ple_gate[i].astype(BF16), w_ple_proj[i].astype(BF16))

    heads = lambda a, b_, s_: a.reshape(b_, s_, N_HEADS, HEAD_DIM)
    conv_p = u_p[:, seq - HIST:]
    conv_s = jnp.concatenate([state_conv[i], u_s], axis=1)[:, -HIST:]
    return (y_p.reshape(batch, seq, D_MODEL), y_s.reshape(nb, dec_seq, D_MODEL),
            heads(k_p, batch, seq), heads(v_p, batch, seq), conv_p,
            heads(k_s, nb, dec_seq), heads(v_s, nb, dec_seq), conv_s)


def kernel(x_prompt, x_sample, cache_k, cache_v, state_conv, page_table, p_prompt, p_sample, g_mix_norm, w_in, g_q, g_k, sb_bias, g_o, w_dw, b_dw, ln_g, ln_b, beta_conv, w_out, g_ffn_norm, w_grp, b_grp, w_rt, b_rt, w_gate, w_up, w_down, g_ple_norm, w_ple_proj, w_ple_gate):
    depth = w_in.shape[0]
    hp, hs = x_prompt, x_sample
    outs = []
    for i in range(depth):
        res = _layer(i, hp, hs, cache_k, cache_v, state_conv, page_table, p_prompt, p_sample,
                     g_mix_norm, w_in, g_q, g_k, sb_bias, g_o, w_dw, b_dw, ln_g, ln_b, beta_conv, w_out,
                     g_ffn_norm, w_grp, b_grp, w_rt, b_rt, w_gate, w_up, w_down,
                     g_ple_norm, w_ple_proj, w_ple_gate)
        hp, hs = res[0], res[1]
        outs.append(res[2:])
    stacked = tuple(jnp.stack([o[n] for o in outs], axis=0) for n in range(6))
    return (hp, hs) + stacked
```
